```python
import jax, jax.numpy as jnp
from jax import lax
import numpy as np

D_MODEL = 2048
BATCH = 4
SEQ = 2048
DEPTH = 2
DEC_BATCH = 16
DEC_SEQ = 64
PAST_LEN = 4096

CHUNK = 64
N_HEADS = D_MODEL // 256
HEAD_DIM = 128
D_ATTN = N_HEADS * HEAD_DIM
D_CONV = D_MODEL // 2
CONV_WIDTH = 31
Q_BLOCK = 128
EPS = 1e-6
D_IN = 4 * D_ATTN + N_HEADS + 3 * D_CONV + 2 * D_MODEL

kernel_name = "fox_conformer_gated_hybrid_stream_step"


def _split_points():
    sizes = (D_ATTN, D_ATTN, D_ATTN, N_HEADS, D_ATTN, 2 * D_CONV, D_CONV, D_MODEL, D_MODEL)
    return np.cumsum(np.array(sizes))[:-1].tolist()


def _rmsnorm(x, g):
    xf = x.astype(jnp.float32)
    y = xf * lax.rsqrt(jnp.mean(xf * xf, axis=-1, keepdims=True) + EPS) * g.astype(jnp.float32)
    return y.astype(x.dtype)


def _layernorm(x, g, b):
    xf = x.astype(jnp.float32)
    mu = jnp.mean(xf, axis=-1, keepdims=True)
    var = jnp.mean(jnp.square(xf - mu), axis=-1, keepdims=True)
    y = (xf - mu) * lax.rsqrt(var + EPS) * g.astype(jnp.float32) + b.astype(jnp.float32)
    return y.astype(x.dtype)


def _fox_prompt(q, k, v, logf):
    B, S, H, Dh = q.shape
    scale = Dh ** -0.5
    F = jnp.cumsum(logf, axis=1).transpose(0, 2, 1)
    nb = S // Q_BLOCK
    qb = q.reshape(B, nb, Q_BLOCK, H, Dh).transpose(1, 0, 2, 3, 4)
    Fb = F.reshape(B, H, nb, Q_BLOCK).transpose(2, 0, 1, 3)
    pos_k = jnp.arange(S)

    def one_block(args):
        i, qi, Fi = args
        s = jnp.einsum('bqhd,bkhd->bhqk', qi, k, preferred_element_type=jnp.float32) * scale
        bias = Fi[..., :, None] - F[..., None, :]
        pos_q = i * Q_BLOCK + jnp.arange(Q_BLOCK)
        mask = pos_k[None, :] <= pos_q[:, None]
        s = jnp.where(mask, s + bias, -jnp.inf)
        p = jax.nn.softmax(s, axis=-1)
        return jnp.einsum('bhqk,bkhd->bqhd', p.astype(v.dtype), v)

    out = lax.map(one_block, (jnp.arange(nb), qb, Fb))
    return out.transpose(1, 0, 2, 3, 4).reshape(B, S, H * Dh)


def _fox_sample(q, k_new, v_new, logf_new, k_cache, v_cache, logf_cache):
    B, T, H, Dh = q.shape
    P = k_cache.shape[1]
    scale = Dh ** -0.5
    k_all = jnp.concatenate([k_cache.astype(k_new.dtype), k_new], axis=1)
    v_all = jnp.concatenate([v_cache.astype(v_new.dtype), v_new], axis=1)
    logf_all = jnp.concatenate([logf_cache.astype(jnp.float32), logf_new], axis=1)
    F = jnp.cumsum(logf_all, axis=1).transpose(0, 2, 1)
    Fq = F[:, :, P:]
    s = jnp.einsum('bqhd,bkhd->bhqk', q, k_all, preferred_element_type=jnp.float32) * scale
    bias = Fq[..., :, None] - F[..., None, :]
    pos_k = jnp.arange(P + T)
    pos_q = P + jnp.arange(T)
    mask = pos_k[None, :] <= pos_q[:, None]
    s = jnp.where(mask, s + bias, -jnp.inf)
    p = jax.nn.softmax(s, axis=-1)
    out = jnp.einsum('bhqk,bkhd->bqhd', p.astype(v_all.dtype), v_all)
    return out.reshape(B, T, H * Dh)


def _depthwise_causal(u, buf, w, b):
    ext = jnp.concatenate([buf.astype(u.dtype), u], axis=1)
    y = lax.conv_general_dilated(ext, w[:, None, :].astype(u.dtype), window_strides=(1,), padding='VALID',
                                 dimension_numbers=('NWC', 'WIO', 'NWC'), feature_group_count=u.shape[-1])
    return y + b, ext[:, -(CONV_WIDTH - 1):]


def _layer(x, c, k_cache, v_cache, logf_cache, conv_buf,
           w_ada, b_ada, norm_g, w_in, b_f, q_g, k_g,
           conv_w, conv_b, ln_g, ln_b, w_oa, w_ob, w_out):
    B, T, _ = x.shape
    mod = jnp.einsum('bd,de->be', jax.nn.silu(c), w_ada) + b_ada
    shift, scale, gate = jnp.split(mod, 3, axis=-1)
    h = _rmsnorm(x, norm_g) * (1 + scale[:, None]) + shift[:, None]
    z = jnp.einsum('btd,de->bte', h, w_in)
    q, k, v, f_logit, g_a, glu_in, g_b, m_a, m_b = jnp.split(z, _split_points(), axis=-1)
    q = _rmsnorm(q.reshape(B, T, N_HEADS, HEAD_DIM), q_g)
    k = _rmsnorm(k.reshape(B, T, N_HEADS, HEAD_DIM), k_g)
    v = v.reshape(B, T, N_HEADS, HEAD_DIM)
    logf = jax.nn.log_sigmoid((f_logit + b_f).astype(jnp.float32))

    if k_cache is None:
        attn = _fox_prompt(q, k, v, logf)
        buf0 = jnp.zeros((B, CONV_WIDTH - 1, D_CONV), x.dtype)
    else:
        attn = _fox_sample(q, k, v, logf, k_cache, v_cache, logf_cache)
        buf0 = conv_buf
    y_a = attn * jax.nn.silu(g_a)

    ga_lin, ga_gate = jnp.split(glu_in, 2, axis=-1)
    u = ga_lin * jax.nn.sigmoid(ga_gate)
    cv, new_buf = _depthwise_causal(u, buf0, conv_w, conv_b)
    cv = jax.nn.silu(_layernorm(cv, ln_g, ln_b))
    y_b = cv * jax.nn.silu(g_b)

    p_a = jnp.einsum('btc,cd->btd', y_a, w_oa)
    p_b = jnp.einsum('btc,cd->btd', y_b, w_ob)
    merged = jax.nn.sigmoid(m_a) * p_a + jax.nn.sigmoid(m_b) * p_b
    out = jnp.einsum('btd,de->bte', merged, w_out)
    x = x + gate[:, None] * out
    return x, k, v, logf, new_buf


def setup_inputs(seed: int = 0) -> dict:
    key = jax.random.key(seed)
    ks = jax.random.split(key, 24)
    n = jax.random.normal
    D = D_MODEL
    return {
        "x_prompt": n(ks[0], (BATCH, SEQ, D), jnp.float32),
        "x_sample": n(ks[1], (DEC_BATCH, DEC_SEQ, D), jnp.float32),
        "c_prompt": n(ks[2], (BATCH, D), jnp.float32),
        "c_sample": n(ks[3], (DEC_BATCH, D), jnp.float32),
        "cache_k": n(ks[4], (DEPTH, DEC_BATCH, PAST_LEN, N_HEADS, HEAD_DIM), jnp.float32),
        "cache_v": n(ks[5], (DEPTH, DEC_BATCH, PAST_LEN, N_HEADS, HEAD_DIM), jnp.float32),
        "cache_logf": jax.nn.log_sigmoid(3.0 + 0.5 * n(ks[6], (DEPTH, DEC_BATCH, PAST_LEN, N_HEADS), jnp.float32)),
        "state_conv": 0.5 * n(ks[7], (DEPTH, DEC_BATCH, CONV_WIDTH - 1, D_CONV), jnp.float32),
        "w_ada": 0.3 * D ** -0.5 * n(ks[8], (DEPTH, D, 3 * D), jnp.float32),
        "b_ada": 0.02 * n(ks[9], (DEPTH, 3 * D), jnp.float32),
        "norm_g": 1.0 + 0.05 * n(ks[10], (DEPTH, D), jnp.float32),
        "w_in": D ** -0.5 * n(ks[11], (DEPTH, D, D_IN), jnp.float32),
        "b_f": 3.0 + 0.5 * n(ks[12], (DEPTH, N_HEADS), jnp.float32),
        "q_norm_g": 1.0 + 0.05 * n(ks[13], (DEPTH, HEAD_DIM), jnp.float32),
        "k_norm_g": 1.0 + 0.05 * n(ks[14], (DEPTH, HEAD_DIM), jnp.float32),
        "conv_w": CONV_WIDTH ** -0.5 * n(ks[15], (DEPTH, CONV_WIDTH, D_CONV), jnp.float32),
        "conv_b": 0.02 * n(ks[16], (DEPTH, D_CONV), jnp.float32),
        "conv_ln_g": 1.0 + 0.05 * n(ks[17], (DEPTH, D_CONV), jnp.float32),
        "conv_ln_b": 0.02 * n(ks[18], (DEPTH, D_CONV), jnp.float32),
        "w_oa": D_ATTN ** -0.5 * n(ks[19], (DEPTH, D_ATTN, D), jnp.float32),
        "w_ob": D_CONV ** -0.5 * n(ks[20], (DEPTH, D_CONV, D), jnp.float32),
        "w_out": D ** -0.5 * n(ks[21], (DEPTH, D, D), jnp.float32),
    }


def reference(x_prompt, x_sample, c_prompt, c_sample, cache_k, cache_v, cache_logf, state_conv,
              w_ada, b_ada, norm_g, w_in, b_f, q_norm_g, k_norm_g, conv_w, conv_b,
              conv_ln_g, conv_ln_b, w_oa, w_ob, w_out):
    xp, xs = x_prompt, x_sample
    kp, vp, fp, cp = [], [], [], []
    ksm, vsm, fsm, csm = [], [], [], []
    for l in range(DEPTH):
        wl = (w_ada[l], b_ada[l], norm_g[l], w_in[l], b_f[l], q_norm_g[l], k_norm_g[l],
              conv_w[l], conv_b[l], conv_ln_g[l], conv_ln_b[l], w_oa[l], w_ob[l], w_out[l])
        xp, k1, v1, f1, b1 = _layer(xp, c_prompt, None, None, None, None, *wl)
        xs, k2, v2, f2, b2 = _layer(xs, c_sample, cache_k[l], cache_v[l], cache_logf[l], state_conv[l], *wl)
        kp.append(k1); vp.append(v1); fp.append(f1); cp.append(b1)
        ksm.append(k2); vsm.append(v2); fsm.append(f2); csm.append(b2)
    return (xp, xs,
            jnp.stack(kp), jnp.stack(vp), jnp.stack(fp), jnp.stack(cp),
            jnp.stack(ksm), jnp.stack(vsm), jnp.stack(fsm), jnp.stack(csm))
```

```python
import functools

import jax
import jax.numpy as jnp
from jax import lax
from jax.experimental import pallas as pl
from jax.experimental.pallas import tpu as pltpu

D_MODEL = 2048
N_HEADS = 8
HEAD_DIM = 128
D_ATTN = N_HEADS * HEAD_DIM
D_CONV = 1024
CONV_WIDTH = 31
HALO = CONV_WIDTH - 1
EPS = 1e-6
NEG = -1e30

LANES = 128
HALO_PAD = 32
CUM_CHUNK = 256
MIB = 1024 * 1024

BF16 = jnp.bfloat16
F32 = jnp.float32


def _params(sem, vmem_mib):
    return pltpu.CompilerParams(dimension_semantics=sem, vmem_limit_bytes=vmem_mib * MIB)


def _const_spec(shape):
    nd = len(shape)
    return pl.BlockSpec(shape, lambda *_: (0,) * nd, pipeline_mode=pl.Buffered(1))


def _sigmoid(x):
    return 1.0 / (1.0 + jnp.exp(-x))


def _silu(x):
    return x * _sigmoid(x)


def _log_sigmoid(x):
    return jnp.minimum(x, 0.0) - jnp.log1p(jnp.exp(-jnp.abs(x)))


def _nt_dot(a, b):
    return lax.dot_general(a, b, (((1,), (1,)), ((), ())), preferred_element_type=F32)


def _dot(a, b):
    return jnp.dot(a, b, preferred_element_type=F32)


def _split3(x):
    hi = x.astype(BF16)
    r1 = x - hi.astype(F32)
    mid = r1.astype(BF16)
    lo = (r1 - mid.astype(F32)).astype(BF16)
    return hi, mid, lo


def _ada_kernel(c_ref, w_ref, b_ref, o_ref):
    s = _silu(c_ref[...]).astype(BF16)
    o_ref[...] = _dot(s, w_ref[...].astype(BF16)) + b_ref[...]


def _ada_call(c_all, w_ada, b_ada):
    depth, d, e = w_ada.shape
    nb = c_all.shape[0]
    tn = 1024
    return pl.pallas_call(
        _ada_kernel,
        grid=(depth, e // tn),
        in_specs=[
            pl.BlockSpec((nb, d), lambda l, n: (0, 0)),
            pl.BlockSpec((None, d, tn), lambda l, n: (l, 0, n)),
            pl.BlockSpec((None, 1, tn), lambda l, n: (l, 0, n)),
        ],
        out_specs=pl.BlockSpec((None, nb, tn), lambda l, n: (l, 0, n)),
        out_shape=jax.ShapeDtypeStruct((depth, nb, e), F32),
        compiler_params=_params(("parallel", "parallel"), 32),
        name="ada_mod",
    )(c_all, w_ada, b_ada.reshape(depth, 1, e))


def _batch_of(row, rows_per_batch, batch0):
    return batch0 + lax.div(row, rows_per_batch)


def _norm_kernel(x_ref, mod_ref, g_ref, h_ref, *, tm, rows_per_batch, batch0):
    m = pl.program_id(0)
    sub = min(tm, rows_per_batch)
    for i in range(tm // sub):
        b = _batch_of(m * tm + i * sub, rows_per_batch, batch0)
        shift = mod_ref[pl.ds(b, 1), 0:D_MODEL]
        scale = mod_ref[pl.ds(b, 1), D_MODEL:2 * D_MODEL]
        x = x_ref[i * sub:(i + 1) * sub, :]
        ms = jnp.mean(x * x, axis=-1, keepdims=True)
        y = x * lax.rsqrt(ms + EPS) * g_ref[...]
        h_ref[i * sub:(i + 1) * sub, :] = (y * (1.0 + scale) + shift).astype(BF16)


def _norm_call(x, mod, g, rows_per_batch, batch0):
    rows = x.shape[0]
    tm = 512
    kern = functools.partial(_norm_kernel, tm=tm, rows_per_batch=rows_per_batch, batch0=batch0)
    return pl.pallas_call(
        kern,
        grid=(rows // tm,),
        in_specs=[
            pl.BlockSpec((tm, D_MODEL), lambda m: (m, 0)),
            _const_spec(mod.shape),
            _const_spec(g.shape),
        ],
        out_specs=pl.BlockSpec((tm, D_MODEL), lambda m: (m, 0)),
        out_shape=jax.ShapeDtypeStruct((rows, D_MODEL), BF16),
        compiler_params=_params(("parallel",), 32),
        name="norm_mod",
    )(x, mod, g)


def _head_rmsnorm_store(acc, g, o_ref, post_scale):
    for hd in range(N_HEADS):
        sl = slice(hd * HEAD_DIM, (hd + 1) * HEAD_DIM)
        a = acc[:, sl]
        ms = jnp.mean(a * a, axis=-1, keepdims=True)
        y = a * lax.rsqrt(ms + EPS) * g
        if post_scale is not None:
            y = y * post_scale
        o_ref[:, sl] = y.astype(o_ref.dtype)


def _in_a_kernel(h_ref, w_ref, wf_ref, bf_ref, qg_ref, kg_ref,
                 lf_ref, q_ref, k_ref, v_ref, ga_ref):
    n = pl.program_id(1)

    @pl.when(n == 0)
    def _():
        f = _dot(h_ref[...], wf_ref[...]) + bf_ref[...]
        lf_ref[...] = _log_sigmoid(f)
        acc = _dot(h_ref[...], w_ref[...])
        _head_rmsnorm_store(acc, qg_ref[...], q_ref, HEAD_DIM ** -0.5)

    @pl.when(n == 1)
    def _():
        acc = _dot(h_ref[...], w_ref[...])
        _head_rmsnorm_store(acc, kg_ref[...], k_ref, None)

    @pl.when(n == 2)
    def _():
        v_ref[...] = _dot(h_ref[...], w_ref[...])

    @pl.when(n == 3)
    def _():
        ga_ref[...] = _silu(_dot(h_ref[...], w_ref[...])).astype(BF16)


def _in_a_call(h, w_a, w_f, b_f, q_g, k_g):
    rows = h.shape[0]
    tm, tn = 1024, 1024
    row_blk = lambda width: pl.BlockSpec((tm, width), lambda m, n: (m, 0))
    return pl.pallas_call(
        _in_a_kernel,
        grid=(rows // tm, 4),
        in_specs=[
            row_blk(D_MODEL),
            pl.BlockSpec((D_MODEL, tn), lambda m, n: (0, n)),
            _const_spec(w_f.shape),
            _const_spec(b_f.shape),
            _const_spec(q_g.shape),
            _const_spec(k_g.shape),
        ],
        out_specs=[row_blk(LANES), row_blk(D_ATTN), row_blk(D_ATTN), row_blk(D_ATTN), row_blk(D_ATTN)],
        out_shape=[
            jax.ShapeDtypeStruct((rows, LANES), F32),
            jax.ShapeDtypeStruct((rows, D_ATTN), BF16),
            jax.ShapeDtypeStruct((rows, D_ATTN), F32),
            jax.ShapeDtypeStruct((rows, D_ATTN), F32),
            jax.ShapeDtypeStruct((rows, D_ATTN), BF16),
        ],
        compiler_params=_params(("parallel", "arbitrary"), 52),
        name="in_proj_a",
    )(h, w_a, w_f, b_f, q_g, k_g)


def _in_b_kernel(h_ref, w_ref, u_ref, gb_ref, mab_ref):
    n = pl.program_id(1)

    @pl.when(n < 2)
    def _():
        acc = _dot(h_ref[...], w_ref[...])
        half = D_CONV // 2
        u_ref[...] = acc[:, :half] * _sigmoid(acc[:, half:])

    @pl.when(n == 2)
    def _():
        gb_ref[...] = _silu(_dot(h_ref[...], w_ref[...])).astype(BF16)

    @pl.when(n > 2)
    def _():
        mab_ref[...] = _sigmoid(_dot(h_ref[...], w_ref[...])).astype(BF16)


def _in_b_call(h, w_b):
    rows = h.shape[0]
    tm, tn = 1024, 1024
    return pl.pallas_call(
        _in_b_kernel,
        grid=(rows // tm, 7),
        in_specs=[
            pl.BlockSpec((tm, D_MODEL), lambda m, n: (m, 0)),
            pl.BlockSpec((D_MODEL, tn), lambda m, n: (0, n)),
        ],
        out_specs=[
            pl.BlockSpec((tm, D_CONV // 2), lambda m, n: (m, jnp.minimum(n, 1))),
            pl.BlockSpec((tm, D_CONV), lambda m, n: (m, 0)),
            pl.BlockSpec((tm, tn), lambda m, n: (m, jnp.clip(n - 3, 0, 3))),
        ],
        out_shape=[
            jax.ShapeDtypeStruct((rows, D_CONV), F32),
            jax.ShapeDtypeStruct((rows, D_CONV), BF16),
            jax.ShapeDtypeStruct((rows, 2 * D_MODEL), BF16),
        ],
        compiler_params=_params(("parallel", "arbitrary"), 44),
        name="in_proj_b",
    )(h, w_b)


def _tri(n, upper):
    r = lax.broadcasted_iota(jnp.int32, (n, n), 0)
    c = lax.broadcasted_iota(jnp.int32, (n, n), 1)
    keep = (r <= c) if upper else (c <= r)
    return jnp.where(keep, 1.0, 0.0).astype(BF16)


def _cum_p_kernel(lf_ref, fc_ref, fr_ref, *, seq):
    tril = _tri(CUM_CHUNK, upper=False)
    carry = jnp.zeros((1, LANES), F32)
    for i in range(seq // CUM_CHUNK):
        sl = slice(i * CUM_CHUNK, (i + 1) * CUM_CHUNK)
        hi, mid, lo = _split3(lf_ref[sl, :])
        y = _dot(tril, hi) + _dot(tril, mid) + _dot(tril, lo) + carry
        carry = y[CUM_CHUNK - 1:CUM_CHUNK, :]
        fc_ref[sl, :] = y
        fr_ref[:, sl] = y.T[0:N_HEADS, :]


def _cum_p_call(lf, batch, seq):
    kern = functools.partial(_cum_p_kernel, seq=seq)
    return pl.pallas_call(
        kern,
        grid=(batch,),
        in_specs=[pl.BlockSpec((seq, LANES), lambda b: (b, 0))],
        out_specs=[
            pl.BlockSpec((seq, LANES), lambda b: (b, 0)),
            pl.BlockSpec((None, N_HEADS, seq), lambda b: (b, 0, 0)),
        ],
        out_shape=[
            jax.ShapeDtypeStruct((batch * seq, LANES), F32),
            jax.ShapeDtypeStruct((batch, N_HEADS, seq), F32),
        ],
        compiler_params=_params(("parallel",), 32),
        name="cum_prompt",
    )(lf)


def _cum_s_kernel(clf_ref, lfn_ref, g_ref, fnc_ref, fnr_ref, f_sc, *, past, new):
    triu = _tri(CUM_CHUNK, upper=True)
    carry = jnp.zeros((N_HEADS, 1), F32)
    for i in range(past // CUM_CHUNK):
        sl = slice(i * CUM_CHUNK, (i + 1) * CUM_CHUNK)
        hi, mid, lo = _split3(clf_ref[:, sl])
        y = _dot(hi, triu) + _dot(mid, triu) + _dot(lo, triu) + carry
        carry = y[:, CUM_CHUNK - 1:CUM_CHUNK]
        f_sc[:, sl] = y
    g_ref[...] = carry - f_sc[...]
    tril = _tri(LANES, upper=False)
    xn = jnp.concatenate([lfn_ref[...], jnp.zeros((LANES - new, LANES), F32)], axis=0)
    hi, mid, lo = _split3(xn)
    yn = _dot(tril, hi) + _dot(tril, mid) + _dot(tril, lo)
    fnc_ref[...] = yn[0:new, :]
    fnr_ref[...] = yn.T[0:N_HEADS, :]


def _cum_s_call(clf_t, lf_new, batch, past, new):
    kern = functools.partial(_cum_s_kernel, past=past, new=new)
    return pl.pallas_call(
        kern,
        grid=(batch,),
        in_specs=[
            pl.BlockSpec((None, N_HEADS, past), lambda b: (b, 0, 0)),
            pl.BlockSpec((new, LANES), lambda b: (b, 0)),
        ],
        out_specs=[
            pl.BlockSpec((None, N_HEADS, past), lambda b: (b, 0, 0)),
            pl.BlockSpec((new, LANES), lambda b: (b, 0)),
            pl.BlockSpec((None, N_HEADS, LANES), lambda b: (b, 0, 0)),
        ],
        out_shape=[
            jax.ShapeDtypeStruct((batch, N_HEADS, past), F32),
            jax.ShapeDtypeStruct((batch * new, LANES), F32),
            jax.ShapeDtypeStruct((batch, N_HEADS, LANES), F32),
        ],
        scratch_shapes=[pltpu.VMEM((N_HEADS, past), F32)],
        compiler_params=_params(("parallel",), 32),
        name="cum_sample",
    )(clf_t, lf_new)


def _softmax_step(carry, s, v):
    m, l, acc = carry
    m_new = jnp.maximum(m, jnp.max(s, axis=-1, keepdims=True))
    alpha = jnp.exp(m - m_new)
    p = jnp.exp(s - m_new)
    l = alpha * l + jnp.sum(p, axis=-1, keepdims=True)
    acc = alpha * acc + _dot(p.astype(BF16), v)
    return m_new, l, acc


def _causal(s):
    row = lax.broadcasted_iota(jnp.int32, s.shape, 0)
    col = lax.broadcasted_iota(jnp.int32, s.shape, 1)
    return jnp.where(col <= row, s, NEG)


def _attn_p_kernel(q_ref, k_ref, v_ref, fc_ref, fr_ref, ga_ref, o_ref, *, tq):
    hd = pl.program_id(1)
    qi = pl.program_id(2)
    q = q_ref[...]
    lane = lax.broadcasted_iota(jnp.int32, (tq, LANES), 1)
    fq = jnp.sum(jnp.where(lane == hd, fc_ref[...], 0.0), axis=-1, keepdims=True)

    def scores(j):
        start = pl.multiple_of(j * tq, tq)
        k = k_ref[pl.ds(start, tq), :].astype(BF16)
        v = v_ref[pl.ds(start, tq), :].astype(BF16)
        fk = fr_ref[pl.ds(hd, 1), pl.ds(start, tq)]
        return _nt_dot(q, k) + fq - fk, v

    def body(j, carry):
        s, v = scores(j)
        return _softmax_step(carry, s, v)

    init = (jnp.full((tq, 1), NEG, F32), jnp.zeros((tq, 1), F32), jnp.zeros((tq, HEAD_DIM), F32))
    carry = lax.fori_loop(0, qi, body, init)
    s, v = scores(qi)
    _, l, acc = _softmax_step(carry, _causal(s), v)
    o_ref[...] = (acc / l * ga_ref[...].astype(F32)).astype(BF16)


def _attn_p_call(q, k, v, f_col, f_row, ga, batch, seq):
    tq = 512
    nq = seq // tq
    tile = pl.BlockSpec((tq, HEAD_DIM), lambda b, h, i: (b * nq + i, h))
    whole = pl.BlockSpec((seq, HEAD_DIM), lambda b, h, i: (b, h))
    return pl.pallas_call(
        functools.partial(_attn_p_kernel, tq=tq),
        grid=(batch, N_HEADS, nq),
        in_specs=[
            tile, whole, whole,
            pl.BlockSpec((tq, LANES), lambda b, h, i: (b * nq + i, 0)),
            pl.BlockSpec((None, N_HEADS, seq), lambda b, h, i: (b, 0, 0)),
            tile,
        ],
        out_specs=tile,
        out_shape=jax.ShapeDtypeStruct((batch * seq, D_ATTN), BF16),
        compiler_params=_params(("parallel", "parallel", "parallel"), 32),
        name="attn_prompt",
    )(q, k, v, f_col, f_row, ga)


def _attn_s_kernel(q_ref, ck_ref, cv_ref, g_ref, fnc_ref, fnr_ref, kn_ref, vn_ref, ga_ref,
                   o_ref, m_sc, l_sc, acc_sc, *, new):
    j = pl.program_id(1)
    last = pl.num_programs(1) - 1

    @pl.when(j == 0)
    def _():
        m_sc[...] = jnp.full(m_sc.shape, NEG, F32)
        l_sc[...] = jnp.zeros(l_sc.shape, F32)
        acc_sc[...] = jnp.zeros(acc_sc.shape, F32)

    def load(hd):
        sl = slice(hd * HEAD_DIM, (hd + 1) * HEAD_DIM)
        return sl, (m_sc[hd][:, 0:1], l_sc[hd][:, 0:1], acc_sc[:, sl])

    def store(hd, sl, carry):
        m, l, acc = carry
        m_sc[hd] = jnp.broadcast_to(m, (new, LANES))
        l_sc[hd] = jnp.broadcast_to(l, (new, LANES))
        acc_sc[:, sl] = acc

    for hd in range(N_HEADS):
        sl, carry = load(hd)
        k = ck_ref[:, sl].astype(BF16)
        v = cv_ref[:, sl].astype(BF16)
        s = _nt_dot(q_ref[:, sl], k) + fnc_ref[:, hd:hd + 1] + g_ref[hd:hd + 1, :]
        store(hd, sl, _softmax_step(carry, s, v))

    @pl.when(j == last)
    def _():
        pad = jnp.zeros((LANES - new, HEAD_DIM), BF16)
        for hd in range(N_HEADS):
            sl, carry = load(hd)
            k = jnp.concatenate([kn_ref[:, sl].astype(BF16), pad], axis=0)
            v = jnp.concatenate([vn_ref[:, sl].astype(BF16), pad], axis=0)
            s = _nt_dot(q_ref[:, sl], k) + fnc_ref[:, hd:hd + 1] - fnr_ref[hd:hd + 1, :]
            _, l, acc = _softmax_step(carry, _causal(s), v)
            o_ref[:, sl] = (acc / l * ga_ref[:, sl].astype(F32)).astype(BF16)


def _attn_s_call(q, cache_k, cache_v, g_row, fn_col, fn_row, k_new, v_new, ga, batch, past, new):
    tk = 1024
    row = lambda width: pl.BlockSpec((new, width), lambda b, j: (b, 0))
    cache = pl.BlockSpec((None, tk, D_ATTN), lambda b, j: (b, j, 0))
    return pl.pallas_call(
        functools.partial(_attn_s_kernel, new=new),
        grid=(batch, past // tk),
        in_specs=[
            row(D_ATTN), cache, cache,
            pl.BlockSpec((None, N_HEADS, tk), lambda b, j: (b, 0, j)),
            row(LANES),
            pl.BlockSpec((None, N_HEADS, LANES), lambda b, j: (b, 0, 0)),
            row(D_ATTN), row(D_ATTN), row(D_ATTN),
        ],
        out_specs=row(D_ATTN),
        out_shape=jax.ShapeDtypeStruct((batch * new, D_ATTN), BF16),
        scratch_shapes=[
            pltpu.VMEM((N_HEADS, new, LANES), F32),
            pltpu.VMEM((N_HEADS, new, LANES), F32),
            pltpu.VMEM((new, D_ATTN), F32),
        ],
        compiler_params=_params(("parallel", "arbitrary"), 40),
        name="attn_sample",
    )(q, cache_k, cache_v, g_row, fn_col, fn_row, k_new, v_new, ga)


CONV_ROWS = 64


def _conv_kernel(u_ref, halo_ref, gb_ref, w_ref, cb_ref, lg_ref, lb_ref, o_ref, ext_sc, cv_sc,
                 *, tt, tiles_per_batch, halo_from_u):
    ext_sc[HALO_PAD:HALO_PAD + tt, :] = u_ref[...]
    if halo_from_u:
        first = lax.rem(pl.program_id(0), tiles_per_batch) == 0

        @pl.when(first)
        def _():
            ext_sc[0:HALO_PAD, :] = jnp.zeros((HALO_PAD, D_CONV), F32)

        @pl.when(jnp.logical_not(first))
        def _():
            ext_sc[0:HALO_PAD, :] = halo_ref[...]
    else:
        ext_sc[HALO_PAD - HALO:HALO_PAD, :] = halo_ref[...]

    base = HALO_PAD - HALO
    for r0 in range(0, tt, CONV_ROWS):
        for c in range(D_CONV // LANES):
            cs = slice(c * LANES, (c + 1) * LANES)
            acc = jnp.zeros((CONV_ROWS, LANES), F32)
            for j in range(CONV_WIDTH):
                acc = acc + w_ref[j:j + 1, cs] * ext_sc[r0 + base + j:r0 + base + j + CONV_ROWS, cs]
            cv_sc[r0:r0 + CONV_ROWS, cs] = acc + cb_ref[:, cs]

    cv = cv_sc[...]
    mu = jnp.mean(cv, axis=-1, keepdims=True)
    d = cv - mu
    var = jnp.mean(d * d, axis=-1, keepdims=True)
    y = d * lax.rsqrt(var + EPS) * lg_ref[...] + lb_ref[...]
    o_ref[...] = (_silu(y) * gb_ref[...].astype(F32)).astype(BF16)


def _conv_call(u, halo, gb, conv_w, conv_b, ln_g, ln_b, tt, tiles_per_batch, halo_from_u):
    rows = u.shape[0]
    tile = pl.BlockSpec((tt, D_CONV), lambda t: (t, 0))
    if halo_from_u:
        per = tt // HALO_PAD
        halo_spec = pl.BlockSpec((HALO_PAD, D_CONV), lambda t: (jnp.maximum(t * per - 1, 0), 0))
    else:
        halo_spec = pl.BlockSpec((None, HALO, D_CONV), lambda t: (t, 0, 0))
    kern = functools.partial(_conv_kernel, tt=tt, tiles_per_batch=tiles_per_batch, halo_from_u=halo_from_u)
    return pl.pallas_call(
        kern,
        grid=(rows // tt,),
        in_specs=[
            tile, halo_spec, tile,
            _const_spec(conv_w.shape), _const_spec(conv_b.shape),
            _const_spec(ln_g.shape), _const_spec(ln_b.shape),
        ],
        out_specs=tile,
        out_shape=jax.ShapeDtypeStruct((rows, D_CONV), BF16),
        scratch_shapes=[
            pltpu.VMEM((HALO_PAD + tt, D_CONV), F32),
            pltpu.VMEM((tt, D_CONV), F32),
        ],
        compiler_params=_params(("parallel",), 32),
        name="conv_module",
    )(u, halo, gb, conv_w, conv_b, ln_g, ln_b)


OUT_COLS = 512


def _out_kernel(ya_ref, yb_ref, mab_ref, x_ref, mod_ref, woa_ref, wob_ref, wout_ref, o_ref, mg_sc,
                *, tm, rows_per_batch, batch0):
    m = pl.program_id(0)
    ya = ya_ref[...]
    yb = yb_ref[...]
    for c in range(D_MODEL // OUT_COLS):
        cs = slice(c * OUT_COLS, (c + 1) * OUT_COLS)
        cs_b = slice(D_MODEL + c * OUT_COLS, D_MODEL + (c + 1) * OUT_COLS)
        pa = _dot(ya, woa_ref[:, cs])
        pb = _dot(yb, wob_ref[:, cs])
        merged = mab_ref[:, cs].astype(F32) * pa + mab_ref[:, cs_b].astype(F32) * pb
        mg_sc[:, cs] = merged.astype(BF16)
    out = _dot(mg_sc[...], wout_ref[...])
    sub = min(tm, rows_per_batch)
    for i in range(tm // sub):
        b = _batch_of(m * tm + i * sub, rows_per_batch, batch0)
        gate = mod_ref[pl.ds(b, 1), 2 * D_MODEL:3 * D_MODEL]
        rs = slice(i * sub, (i + 1) * sub)
        o_ref[rs, :] = x_ref[rs, :] + gate * out[rs, :]


def _out_call(ya, yb, mab, x, mod, w_oa, w_ob, w_out, rows_per_batch, batch0):
    rows = x.shape[0]
    tm = 512
    row_blk = lambda width: pl.BlockSpec((tm, width), lambda m: (m, 0))
    kern = functools.partial(_out_kernel, tm=tm, rows_per_batch=rows_per_batch, batch0=batch0)
    return pl.pallas_call(
        kern,
        grid=(rows // tm,),
        in_specs=[
            row_blk(D_ATTN), row_blk(D_CONV), row_blk(2 * D_MODEL), row_blk(D_MODEL),
            _const_spec(mod.shape), _const_spec(w_oa.shape), _const_spec(w_ob.shape), _const_spec(w_out.shape),
        ],
        out_specs=row_blk(D_MODEL),
        out_shape=jax.ShapeDtypeStruct((rows, D_MODEL), F32),
        scratch_shapes=[pltpu.VMEM((tm, D_MODEL), BF16)],
        compiler_params=_params(("parallel",), 56),
        name="out_proj",
    )(ya, yb, mab, x, mod, w_oa, w_ob, w_out)


def _prep_w_in(w):
    o_f = 3 * D_ATTN
    o_ga = o_f + N_HEADS
    o_glu = o_ga + D_ATTN
    o_gb = o_glu + 2 * D_CONV
    o_m = o_gb + D_CONV
    half = D_CONV // 2
    lin = w[:, o_glu:o_glu + D_CONV]
    gate = w[:, o_glu + D_CONV:o_gb]
    w_a = jnp.concatenate([w[:, :o_f], w[:, o_ga:o_glu]], axis=1).astype(BF16)
    w_b = jnp.concatenate([lin[:, :half], gate[:, :half], lin[:, half:], gate[:, half:],
                           w[:, o_gb:o_m], w[:, o_m:]], axis=1).astype(BF16)
    w_f = jnp.pad(w[:, o_f:o_ga], ((0, 0), (0, LANES - N_HEADS))).astype(BF16)
    return w_a, w_b, w_f


def _layer_group(x, mod, rows_per_batch, batch0, wl):
    h = _norm_call(x, mod, wl["norm_g"], rows_per_batch, batch0)
    lf, q, k, v, ga = _in_a_call(h, wl["w_a"], wl["w_f"], wl["b_f"], wl["q_g"], wl["k_g"])
    u, gb, mab = _in_b_call(h, wl["w_b"])
    return lf, q, k, v, ga, u, gb, mab


def kernel(x_prompt, x_sample, c_prompt, c_sample, cache_k, cache_v, cache_logf, state_conv, w_ada, b_ada, norm_g, w_in, b_f, q_norm_g, k_norm_g, conv_w, conv_b, conv_ln_g, conv_ln_b, w_oa, w_ob, w_out):
    batch, seq, d = x_prompt.shape
    dec_batch, dec_seq, _ = x_sample.shape
    depth = w_ada.shape[0]
    past = cache_k.shape[2]
    assert d == D_MODEL and seq % 1024 == 0 and (dec_batch * dec_seq) % 1024 == 0
    assert dec_seq >= HALO and seq >= HALO and dec_seq <= LANES

    nb = batch + dec_batch
    nb_pad = -(-nb // 8) * 8
    c_all = jnp.concatenate([c_prompt, c_sample, jnp.zeros((nb_pad - nb, d), F32)], axis=0)
    mod_all = _ada_call(c_all, w_ada, b_ada)

    xp = x_prompt.reshape(batch * seq, d)
    xs = x_sample.reshape(dec_batch * dec_seq, d)
    cache_k2 = cache_k.reshape(depth, dec_batch, past, D_ATTN)
    cache_v2 = cache_v.reshape(depth, dec_batch, past, D_ATTN)
    cache_lf_t = jnp.swapaxes(cache_logf, 2, 3)

    outs = {name: [] for name in ("kp", "vp", "fp", "cp", "ks", "vs", "fs", "cs")}
    for l in range(depth):
        w_a, w_b, w_f = _prep_w_in(w_in[l])
        wl = dict(
            norm_g=norm_g[l].reshape(1, d), w_a=w_a, w_b=w_b, w_f=w_f,
            b_f=jnp.pad(b_f[l], (0, LANES - N_HEADS)).reshape(1, LANES),
            q_g=q_norm_g[l].reshape(1, HEAD_DIM), k_g=k_norm_g[l].reshape(1, HEAD_DIM),
        )
        cw, cb = conv_w[l], conv_b[l].reshape(1, D_CONV)
        lg, lb = conv_ln_g[l].reshape(1, D_CONV), conv_ln_b[l].reshape(1, D_CONV)
        woa, wob, wout = w_oa[l].astype(BF16), w_ob[l].astype(BF16), w_out[l].astype(BF16)
        mod = mod_all[l]

        lf, q, k, v, ga, u, gb, mab = _layer_group(xp, mod, seq, 0, wl)
        f_col, f_row = _cum_p_call(lf, batch, seq)
        ya = _attn_p_call(q, k, v, f_col, f_row, ga, batch, seq)
        yb = _conv_call(u, u, gb, cw, cb, lg, lb, 256, seq // 256, True)
        xp = _out_call(ya, yb, mab, xp, mod, woa, wob, wout, seq, 0)
        outs["kp"].append(k.reshape(batch, seq, N_HEADS, HEAD_DIM))
        outs["vp"].append(v.reshape(batch, seq, N_HEADS, HEAD_DIM))
        outs["fp"].append(lf[:, :N_HEADS].reshape(batch, seq, N_HEADS))
        outs["cp"].append(u.reshape(batch, seq, D_CONV)[:, seq - HALO:, :])

        lf, q, k, v, ga, u, gb, mab = _layer_group(xs, mod, dec_seq, batch, wl)
        g_row, fn_col, fn_row = _cum_s_call(cache_lf_t[l], lf, dec_batch, past, dec_seq)
        ya = _attn_s_call(q, cache_k2[l], cache_v2[l], g_row, fn_col, fn_row, k, v, ga,
                          dec_batch, past, dec_seq)
        yb = _conv_call(u, state_conv[l], gb, cw, cb, lg, lb, dec_seq, 1, False)
        xs = _out_call(ya, yb, mab, xs, mod, woa, wob, wout, dec_seq, batch)
        outs["ks"].append(k.reshape(dec_batch, dec_seq, N_HEADS, HEAD_DIM))
        outs["vs"].append(v.reshape(dec_batch, dec_seq, N_HEADS, HEAD_DIM))
        outs["fs"].append(lf[:, :N_HEADS].reshape(dec_batch, dec_seq, N_HEADS))
        outs["cs"].append(u.reshape(dec_batch, dec_seq, D_CONV)[:, dec_seq - HALO:, :])

    st = {name: jnp.stack(vals) for name, vals in outs.items()}
    return (xp.reshape(batch, seq, d), xs.reshape(dec_batch, dec_seq, d),
            st["kp"], st["vp"], st["fp"], st["cp"], st["ks"], st["vs"], st["fs"], st["cs"])
```

```python
import functools

import jax
import jax.numpy as jnp
from jax import lax
from jax.experimental import pallas as pl
from jax.experimental.pallas import tpu as pltpu

D_MODEL = 2048
N_HEADS = 8
HEAD_DIM = 128
D_ATTN = N_HEADS * HEAD_DIM
D_CONV = 1024
CONV_WIDTH = 31
HALO = CONV_WIDTH - 1
EPS = 1e-6
NEG = -1e30

LANES = 128
HALO_PAD = 32
CUM_CHUNK = 256
MIB = 1024 * 1024

BF16 = jnp.bfloat16
F32 = jnp.float32


def _params(sem, vmem_mib):
    return pltpu.CompilerParams(dimension_semantics=sem, vmem_limit_bytes=vmem_mib * MIB)


def _const_spec(shape):
    nd = len(shape)
    return pl.BlockSpec(shape, lambda *_: (0,) * nd, pipeline_mode=pl.Buffered(1))


def _sigmoid(x):
    return 1.0 / (1.0 + jnp.exp(-x))


def _silu(x):
    return x * _sigmoid(x)


def _log_sigmoid(x):
    return jnp.minimum(x, 0.0) - jnp.log1p(jnp.exp(-jnp.abs(x)))


def _nt_dot(a, b):
    return lax.dot_general(a, b, (((1,), (1,)), ((), ())), preferred_element_type=F32)


def _dot(a, b):
    return jnp.dot(a, b, preferred_element_type=F32)


def _split3(x):
    hi = x.astype(BF16)
    r1 = x - hi.astype(F32)
    mid = r1.astype(BF16)
    lo = (r1 - mid.astype(F32)).astype(BF16)
    return hi, mid, lo


def _ada_kernel(c_ref, w_ref, b_ref, o_ref):
    s = _silu(c_ref[...]).astype(BF16)
    o_ref[...] = _dot(s, w_ref[...].astype(BF16)) + b_ref[...]


def _ada_call(c_all, w_ada, b_ada):
    depth, d, e = w_ada.shape
    nb = c_all.shape[0]
    tn = 1024
    return pl.pallas_call(
        _ada_kernel,
        grid=(depth, e // tn),
        in_specs=[
            pl.BlockSpec((nb, d), lambda l, n: (0, 0)),
            pl.BlockSpec((None, d, tn), lambda l, n: (l, 0, n)),
            pl.BlockSpec((None, 1, tn), lambda l, n: (l, 0, n)),
        ],
        out_specs=pl.BlockSpec((None, nb, tn), lambda l, n: (l, 0, n)),
        out_shape=jax.ShapeDtypeStruct((depth, nb, e), F32),
        compiler_params=_params(("parallel", "parallel"), 32),
        name="ada_mod",
    )(c_all, w_ada, b_ada.reshape(depth, 1, e))


def _batch_of(row, rows_per_batch, batch0):
    return batch0 + lax.div(row, rows_per_batch)


def _norm_kernel(x_ref, mod_ref, g_ref, h_ref, *, tm, rows_per_batch, batch0):
    m = pl.program_id(0)
    sub = min(tm, rows_per_batch)
    for i in range(tm // sub):
        b = _batch_of(m * tm + i * sub, rows_per_batch, batch0)
        shift = mod_ref[pl.ds(b, 1), 0:D_MODEL]
        scale = mod_ref[pl.ds(b, 1), D_MODEL:2 * D_MODEL]
        x = x_ref[i * sub:(i + 1) * sub, :]
        ms = jnp.mean(x * x, axis=-1, keepdims=True)
        y = x * lax.rsqrt(ms + EPS) * g_ref[...]
        h_ref[i * sub:(i + 1) * sub, :] = (y * (1.0 + scale) + shift).astype(BF16)


def _norm_call(x, mod, g, rows_per_batch, batch0):
    rows = x.shape[0]
    tm = 512
    kern = functools.partial(_norm_kernel, tm=tm, rows_per_batch=rows_per_batch, batch0=batch0)
    return pl.pallas_call(
        kern,
        grid=(rows // tm,),
        in_specs=[
            pl.BlockSpec((tm, D_MODEL), lambda m: (m, 0)),
            _const_spec(mod.shape),
            _const_spec(g.shape),
        ],
        out_specs=pl.BlockSpec((tm, D_MODEL), lambda m: (m, 0)),
        out_shape=jax.ShapeDtypeStruct((rows, D_MODEL), BF16),
        compiler_params=_params(("parallel",), 32),
        name="norm_mod",
    )(x, mod, g)


def _head_rmsnorm(a, g):
    ms = jnp.mean(a * a, axis=-1, keepdims=True)
    return a * lax.rsqrt(ms + EPS) * g


def _store_heads(y, hd, tm, cache_ref, bf_ref):
    cache_ref[pl.ds(hd, tm, stride=N_HEADS), :] = y
    bf_ref[:, hd * HEAD_DIM:(hd + 1) * HEAD_DIM] = y.astype(BF16)


def _in_a_kernel(*refs, tm, n_alias):
    (h_ref, w_ref, wf_ref, bf_ref, qg_ref, kg_ref) = refs[:6]
    (lf_ref, q_ref, k_ref, v_ref, kb_ref, vb_ref, ga_ref) = refs[6 + n_alias:]
    n = pl.program_id(1)

    def head(acc, hd):
        return acc[:, hd * HEAD_DIM:(hd + 1) * HEAD_DIM]

    @pl.when(n == 0)
    def _():
        f = _dot(h_ref[...], wf_ref[...]) + bf_ref[...]
        lf_ref[...] = _log_sigmoid(f)
        acc = _dot(h_ref[...], w_ref[...])
        for hd in range(N_HEADS):
            y = _head_rmsnorm(head(acc, hd), qg_ref[...]) * (HEAD_DIM ** -0.5)
            q_ref[:, hd * HEAD_DIM:(hd + 1) * HEAD_DIM] = y.astype(BF16)

    @pl.when(n == 1)
    def _():
        acc = _dot(h_ref[...], w_ref[...])
        for hd in range(N_HEADS):
            _store_heads(_head_rmsnorm(head(acc, hd), kg_ref[...]), hd, tm, k_ref, kb_ref)

    @pl.when(n == 2)
    def _():
        acc = _dot(h_ref[...], w_ref[...])
        for hd in range(N_HEADS):
            _store_heads(head(acc, hd), hd, tm, v_ref, vb_ref)

    @pl.when(n == 3)
    def _():
        ga_ref[...] = _silu(_dot(h_ref[...], w_ref[...])).astype(BF16)


def _in_a_call(h, w_a, w_f, b_f, q_g, k_g, layer, depth, kv_prev):
    rows = h.shape[0]
    tm, tn = 1024, 1024
    row_blk = lambda width: pl.BlockSpec((tm, width), lambda m, n: (m, 0))
    cache_blk = pl.BlockSpec((None, tm * N_HEADS, HEAD_DIM), lambda m, n: (layer, m, 0))
    cache_shape = jax.ShapeDtypeStruct((depth, rows * N_HEADS, HEAD_DIM), F32)
    alias_in = [] if kv_prev is None else list(kv_prev)
    n_in = 6
    return pl.pallas_call(
        functools.partial(_in_a_kernel, tm=tm, n_alias=len(alias_in)),
        grid=(rows // tm, 4),
        in_specs=[
            row_blk(D_MODEL),
            pl.BlockSpec((D_MODEL, tn), lambda m, n: (0, n)),
            _const_spec(w_f.shape),
            _const_spec(b_f.shape),
            _const_spec(q_g.shape),
            _const_spec(k_g.shape),
        ] + [pl.BlockSpec(memory_space=pl.ANY)] * len(alias_in),
        out_specs=[row_blk(LANES), row_blk(D_ATTN), cache_blk, cache_blk,
                   row_blk(D_ATTN), row_blk(D_ATTN), row_blk(D_ATTN)],
        out_shape=[
            jax.ShapeDtypeStruct((rows, LANES), F32),
            jax.ShapeDtypeStruct((rows, D_ATTN), BF16),
            cache_shape,
            cache_shape,
            jax.ShapeDtypeStruct((rows, D_ATTN), BF16),
            jax.ShapeDtypeStruct((rows, D_ATTN), BF16),
            jax.ShapeDtypeStruct((rows, D_ATTN), BF16),
        ],
        input_output_aliases={n_in + i: 2 + i for i in range(len(alias_in))},
        compiler_params=_params(("parallel", "arbitrary"), 58),
        name="in_proj_a",
    )(h, w_a, w_f, b_f, q_g, k_g, *alias_in)


def _in_b_kernel(h_ref, w_ref, u_ref, gb_ref, mab_ref):
    n = pl.program_id(1)

    @pl.when(n < 2)
    def _():
        acc = _dot(h_ref[...], w_ref[...])
        half = D_CONV // 2
        u_ref[...] = acc[:, :half] * _sigmoid(acc[:, half:])

    @pl.when(n == 2)
    def _():
        gb_ref[...] = _silu(_dot(h_ref[...], w_ref[...])).astype(BF16)

    @pl.when(n > 2)
    def _():
        mab_ref[...] = _sigmoid(_dot(h_ref[...], w_ref[...])).astype(BF16)


def _in_b_call(h, w_b):
    rows = h.shape[0]
    tm, tn = 1024, 1024
    return pl.pallas_call(
        _in_b_kernel,
        grid=(rows // tm, 7),
        in_specs=[
            pl.BlockSpec((tm, D_MODEL), lambda m, n: (m, 0)),
            pl.BlockSpec((D_MODEL, tn), lambda m, n: (0, n)),
        ],
        out_specs=[
            pl.BlockSpec((tm, D_CONV // 2), lambda m, n: (m, jnp.minimum(n, 1))),
            pl.BlockSpec((tm, D_CONV), lambda m, n: (m, 0)),
            pl.BlockSpec((tm, tn), lambda m, n: (m, jnp.clip(n - 3, 0, 3))),
        ],
        out_shape=[
            jax.ShapeDtypeStruct((rows, D_CONV), F32),
            jax.ShapeDtypeStruct((rows, D_CONV), BF16),
            jax.ShapeDtypeStruct((rows, 2 * D_MODEL), BF16),
        ],
        compiler_params=_params(("parallel", "arbitrary"), 44),
        name="in_proj_b",
    )(h, w_b)


def _tri(n, upper):
    r = lax.broadcasted_iota(jnp.int32, (n, n), 0)
    c = lax.broadcasted_iota(jnp.int32, (n, n), 1)
    keep = (r <= c) if upper else (c <= r)
    return jnp.where(keep, 1.0, 0.0).astype(BF16)


def _cum_p_kernel(lf_ref, fc_ref, fr_ref, *, seq):
    tril = _tri(CUM_CHUNK, upper=False)
    carry = jnp.zeros((1, LANES), F32)
    for i in range(seq // CUM_CHUNK):
        sl = slice(i * CUM_CHUNK, (i + 1) * CUM_CHUNK)
        hi, mid, lo = _split3(lf_ref[sl, :])
        y = _dot(tril, hi) + _dot(tril, mid) + _dot(tril, lo) + carry
        carry = y[CUM_CHUNK - 1:CUM_CHUNK, :]
        fc_ref[sl, :] = y
        fr_ref[:, sl] = y.T[0:N_HEADS, :]


def _cum_p_call(lf, batch, seq):
    kern = functools.partial(_cum_p_kernel, seq=seq)
    return pl.pallas_call(
        kern,
        grid=(batch,),
        in_specs=[pl.BlockSpec((seq, LANES), lambda b: (b, 0))],
        out_specs=[
            pl.BlockSpec((seq, LANES), lambda b: (b, 0)),
            pl.BlockSpec((None, N_HEADS, seq), lambda b: (b, 0, 0)),
        ],
        out_shape=[
            jax.ShapeDtypeStruct((batch * seq, LANES), F32),
            jax.ShapeDtypeStruct((batch, N_HEADS, seq), F32),
        ],
        compiler_params=_params(("parallel",), 32),
        name="cum_prompt",
    )(lf)


def _cum_s_kernel(clf_ref, lfn_ref, g_ref, fnc_ref, fnr_ref, f_sc, *, past, new):
    triu = _tri(CUM_CHUNK, upper=True)
    carry = jnp.zeros((N_HEADS, 1), F32)
    for i in range(past // CUM_CHUNK):
        sl = slice(i * CUM_CHUNK, (i + 1) * CUM_CHUNK)
        hi, mid, lo = _split3(clf_ref[:, sl])
        y = _dot(hi, triu) + _dot(mid, triu) + _dot(lo, triu) + carry
        carry = y[:, CUM_CHUNK - 1:CUM_CHUNK]
        f_sc[:, sl] = y
    g_ref[...] = carry - f_sc[...]
    tril = _tri(LANES, upper=False)
    xn = jnp.concatenate([lfn_ref[...], jnp.zeros((LANES - new, LANES), F32)], axis=0)
    hi, mid, lo = _split3(xn)
    yn = _dot(tril, hi) + _dot(tril, mid) + _dot(tril, lo)
    fnc_ref[...] = yn[0:new, :]
    fnr_ref[...] = yn.T[0:N_HEADS, :]


def _cum_s_call(clf_t, lf_new, batch, past, new):
    kern = functools.partial(_cum_s_kernel, past=past, new=new)
    return pl.pallas_call(
        kern,
        grid=(batch,),
        in_specs=[
            pl.BlockSpec((None, N_HEADS, past), lambda b: (b, 0, 0)),
            pl.BlockSpec((new, LANES), lambda b: (b, 0)),
        ],
        out_specs=[
            pl.BlockSpec((None, N_HEADS, past), lambda b: (b, 0, 0)),
            pl.BlockSpec((new, LANES), lambda b: (b, 0)),
            pl.BlockSpec((None, N_HEADS, LANES), lambda b: (b, 0, 0)),
        ],
        out_shape=[
            jax.ShapeDtypeStruct((batch, N_HEADS, past), F32),
            jax.ShapeDtypeStruct((batch * new, LANES), F32),
            jax.ShapeDtypeStruct((batch, N_HEADS, LANES), F32),
        ],
        scratch_shapes=[pltpu.VMEM((N_HEADS, past), F32)],
        compiler_params=_params(("parallel",), 32),
        name="cum_sample",
    )(clf_t, lf_new)


def _softmax_step(carry, s, v):
    m, l, acc = carry
    m_new = jnp.maximum(m, jnp.max(s, axis=-1, keepdims=True))
    alpha = jnp.exp(m - m_new)
    p = jnp.exp(s - m_new)
    l = alpha * l + jnp.sum(p, axis=-1, keepdims=True)
    acc = alpha * acc + _dot(p.astype(BF16), v)
    return m_new, l, acc


def _causal(s):
    row = lax.broadcasted_iota(jnp.int32, s.shape, 0)
    col = lax.broadcasted_iota(jnp.int32, s.shape, 1)
    return jnp.where(col <= row, s, NEG)


def _attn_p_kernel(q_ref, k_ref, v_ref, fc_ref, fr_ref, ga_ref, o_ref, *, tq):
    hd = pl.program_id(1)
    qi = pl.program_id(2)
    q = q_ref[...]
    lane = lax.broadcasted_iota(jnp.int32, (tq, LANES), 1)
    fq = jnp.sum(jnp.where(lane == hd, fc_ref[...], 0.0), axis=-1, keepdims=True)

    def scores(j):
        start = pl.multiple_of(j * tq, tq)
        k = k_ref[pl.ds(start, tq), :]
        v = v_ref[pl.ds(start, tq), :]
        fk = fr_ref[pl.ds(hd, 1), pl.ds(start, tq)]
        return _nt_dot(q, k) + fq - fk, v

    def body(j, carry):
        s, v = scores(j)
        return _softmax_step(carry, s, v)

    init = (jnp.full((tq, 1), NEG, F32), jnp.zeros((tq, 1), F32), jnp.zeros((tq, HEAD_DIM), F32))
    carry = lax.fori_loop(0, qi, body, init)
    s, v = scores(qi)
    _, l, acc = _softmax_step(carry, _causal(s), v)
    o_ref[...] = (acc / l * ga_ref[...].astype(F32)).astype(BF16)


def _attn_p_call(q, k, v, f_col, f_row, ga, batch, seq):
    tq = 512
    nq = seq // tq
    tile = pl.BlockSpec((tq, HEAD_DIM), lambda b, h, i: (b * nq + i, h))
    whole = pl.BlockSpec((seq, HEAD_DIM), lambda b, h, i: (b, h))
    return pl.pallas_call(
        functools.partial(_attn_p_kernel, tq=tq),
        grid=(batch, N_HEADS, nq),
        in_specs=[
            tile, whole, whole,
            pl.BlockSpec((tq, LANES), lambda b, h, i: (b * nq + i, 0)),
            pl.BlockSpec((None, N_HEADS, seq), lambda b, h, i: (b, 0, 0)),
            tile,
        ],
        out_specs=tile,
        out_shape=jax.ShapeDtypeStruct((batch * seq, D_ATTN), BF16),
        compiler_params=_params(("parallel", "parallel", "parallel"), 32),
        name="attn_prompt",
    )(q, k, v, f_col, f_row, ga)


def _attn_s_kernel(q_ref, ck_ref, cv_ref, g_ref, fnc_ref, fnr_ref, kn_ref, vn_ref, ga_ref,
                   o_ref, m_sc, l_sc, acc_sc, *, new, tk):
    j = pl.program_id(1)
    last = pl.num_programs(1) - 1

    @pl.when(j == 0)
    def _():
        m_sc[...] = jnp.full(m_sc.shape, NEG, F32)
        l_sc[...] = jnp.zeros(l_sc.shape, F32)
        acc_sc[...] = jnp.zeros(acc_sc.shape, F32)

    def load(hd):
        sl = slice(hd * HEAD_DIM, (hd + 1) * HEAD_DIM)
        return sl, (m_sc[hd][:, 0:1], l_sc[hd][:, 0:1], acc_sc[:, sl])

    def store(hd, sl, carry):
        m, l, acc = carry
        m_sc[hd] = jnp.broadcast_to(m, (new, LANES))
        l_sc[hd] = jnp.broadcast_to(l, (new, LANES))
        acc_sc[:, sl] = acc

    for hd in range(N_HEADS):
        sl, carry = load(hd)
        k = ck_ref[pl.ds(hd, tk, stride=N_HEADS), :].astype(BF16)
        v = cv_ref[pl.ds(hd, tk, stride=N_HEADS), :].astype(BF16)
        s = _nt_dot(q_ref[:, sl], k) + fnc_ref[:, hd:hd + 1] + g_ref[hd:hd + 1, :]
        store(hd, sl, _softmax_step(carry, s, v))

    @pl.when(j == last)
    def _():
        pad = jnp.zeros((LANES - new, HEAD_DIM), BF16)
        for hd in range(N_HEADS):
            sl, carry = load(hd)
            k = jnp.concatenate([kn_ref[:, sl], pad], axis=0)
            v = jnp.concatenate([vn_ref[:, sl], pad], axis=0)
            s = _nt_dot(q_ref[:, sl], k) + fnc_ref[:, hd:hd + 1] - fnr_ref[hd:hd + 1, :]
            _, l, acc = _softmax_step(carry, _causal(s), v)
            o_ref[:, sl] = (acc / l * ga_ref[:, sl].astype(F32)).astype(BF16)


def _attn_s_call(q, cache_k, cache_v, g_row, fn_col, fn_row, k_new, v_new, ga, layer, batch, past, new):
    tk = 1024
    row = lambda width: pl.BlockSpec((new, width), lambda b, j: (b, 0))
    cache = pl.BlockSpec((None, None, tk * N_HEADS, HEAD_DIM), lambda b, j: (layer, b, j, 0))
    return pl.pallas_call(
        functools.partial(_attn_s_kernel, new=new, tk=tk),
        grid=(batch, past // tk),
        in_specs=[
            row(D_ATTN), cache, cache,
            pl.BlockSpec((None, N_HEADS, tk), lambda b, j: (b, 0, j)),
            row(LANES),
            pl.BlockSpec((None, N_HEADS, LANES), lambda b, j: (b, 0, 0)),
            row(D_ATTN), row(D_ATTN), row(D_ATTN),
        ],
        out_specs=row(D_ATTN),
        out_shape=jax.ShapeDtypeStruct((batch * new, D_ATTN), BF16),
        scratch_shapes=[
            pltpu.VMEM((N_HEADS, new, LANES), F32),
            pltpu.VMEM((N_HEADS, new, LANES), F32),
            pltpu.VMEM((new, D_ATTN), F32),
        ],
        compiler_params=_params(("parallel", "arbitrary"), 40),
        name="attn_sample",
    )(q, cache_k, cache_v, g_row, fn_col, fn_row, k_new, v_new, ga)


CONV_ROWS = 64


def _conv_kernel(u_ref, halo_ref, gb_ref, w_ref, cb_ref, lg_ref, lb_ref, o_ref, ext_sc, cv_sc,
                 *, tt, tiles_per_batch, halo_from_u):
    ext_sc[HALO_PAD:HALO_PAD + tt, :] = u_ref[...]
    if halo_from_u:
        first = lax.rem(pl.program_id(0), tiles_per_batch) == 0

        @pl.when(first)
        def _():
            ext_sc[0:HALO_PAD, :] = jnp.zeros((HALO_PAD, D_CONV), F32)

        @pl.when(jnp.logical_not(first))
        def _():
            ext_sc[0:HALO_PAD, :] = halo_ref[...]
    else:
        ext_sc[HALO_PAD - HALO:HALO_PAD, :] = halo_ref[...]

    base = HALO_PAD - HALO
    for r0 in range(0, tt, CONV_ROWS):
        for c in range(D_CONV // LANES):
            cs = slice(c * LANES, (c + 1) * LANES)
            acc = jnp.zeros((CONV_ROWS, LANES), F32)
            for j in range(CONV_WIDTH):
                acc = acc + w_ref[j:j + 1, cs] * ext_sc[r0 + base + j:r0 + base + j + CONV_ROWS, cs]
            cv_sc[r0:r0 + CONV_ROWS, cs] = acc + cb_ref[:, cs]

    cv = cv_sc[...]
    mu = jnp.mean(cv, axis=-1, keepdims=True)
    d = cv - mu
    var = jnp.mean(d * d, axis=-1, keepdims=True)
    y = d * lax.rsqrt(var + EPS) * lg_ref[...] + lb_ref[...]
    o_ref[...] = (_silu(y) * gb_ref[...].astype(F32)).astype(BF16)


def _conv_call(u, halo, gb, conv_w, conv_b, ln_g, ln_b, tt, tiles_per_batch, halo_from_u):
    rows = u.shape[0]
    tile = pl.BlockSpec((tt, D_CONV), lambda t: (t, 0))
    if halo_from_u:
        per = tt // HALO_PAD
        halo_spec = pl.BlockSpec((HALO_PAD, D_CONV), lambda t: (jnp.maximum(t * per - 1, 0), 0))
    else:
        halo_spec = pl.BlockSpec((None, HALO, D_CONV), lambda t: (t, 0, 0))
    kern = functools.partial(_conv_kernel, tt=tt, tiles_per_batch=tiles_per_batch, halo_from_u=halo_from_u)
    return pl.pallas_call(
        kern,
        grid=(rows // tt,),
        in_specs=[
            tile, halo_spec, tile,
            _const_spec(conv_w.shape), _const_spec(conv_b.shape),
            _const_spec(ln_g.shape), _const_spec(ln_b.shape),
        ],
        out_specs=tile,
        out_shape=jax.ShapeDtypeStruct((rows, D_CONV), BF16),
        scratch_shapes=[
            pltpu.VMEM((HALO_PAD + tt, D_CONV), F32),
            pltpu.VMEM((tt, D_CONV), F32),
        ],
        compiler_params=_params(("parallel",), 32),
        name="conv_module",
    )(u, halo, gb, conv_w, conv_b, ln_g, ln_b)


OUT_COLS = 512


def _out_kernel(ya_ref, yb_ref, mab_ref, x_ref, mod_ref, woa_ref, wob_ref, wout_ref, o_ref, mg_sc,
                *, tm, rows_per_batch, batch0):
    m = pl.program_id(0)
    ya = ya_ref[...]
    yb = yb_ref[...]
    for c in range(D_MODEL // OUT_COLS):
        cs = slice(c * OUT_COLS, (c + 1) * OUT_COLS)
        cs_b = slice(D_MODEL + c * OUT_COLS, D_MODEL + (c + 1) * OUT_COLS)
        pa = _dot(ya, woa_ref[:, cs])
        pb = _dot(yb, wob_ref[:, cs])
        merged = mab_ref[:, cs].astype(F32) * pa + mab_ref[:, cs_b].astype(F32) * pb
        mg_sc[:, cs] = merged.astype(BF16)
    out = _dot(mg_sc[...], wout_ref[...])
    sub = min(tm, rows_per_batch)
    for i in range(tm // sub):
        b = _batch_of(m * tm + i * sub, rows_per_batch, batch0)
        gate = mod_ref[pl.ds(b, 1), 2 * D_MODEL:3 * D_MODEL]
        rs = slice(i * sub, (i + 1) * sub)
        o_ref[rs, :] = x_ref[rs, :] + gate * out[rs, :]


def _out_call(ya, yb, mab, x, mod, w_oa, w_ob, w_out, rows_per_batch, batch0):
    rows = x.shape[0]
    tm = 512
    row_blk = lambda width: pl.BlockSpec((tm, width), lambda m: (m, 0))
    kern = functools.partial(_out_kernel, tm=tm, rows_per_batch=rows_per_batch, batch0=batch0)
    return pl.pallas_call(
        kern,
        grid=(rows // tm,),
        in_specs=[
            row_blk(D_ATTN), row_blk(D_CONV), row_blk(2 * D_MODEL), row_blk(D_MODEL),
            _const_spec(mod.shape), _const_spec(w_oa.shape), _const_spec(w_ob.shape), _const_spec(w_out.shape),
        ],
        out_specs=row_blk(D_MODEL),
        out_shape=jax.ShapeDtypeStruct((rows, D_MODEL), F32),
        scratch_shapes=[pltpu.VMEM((tm, D_MODEL), BF16)],
        compiler_params=_params(("parallel",), 56),
        name="out_proj",
    )(ya, yb, mab, x, mod, w_oa, w_ob, w_out)


def _prep_w_in(w):
    o_f = 3 * D_ATTN
    o_ga = o_f + N_HEADS
    o_glu = o_ga + D_ATTN
    o_gb = o_glu + 2 * D_CONV
    o_m = o_gb + D_CONV
    half = D_CONV // 2
    lin = w[:, o_glu:o_glu + D_CONV]
    gate = w[:, o_glu + D_CONV:o_gb]
    w_a = jnp.concatenate([w[:, :o_f], w[:, o_ga:o_glu]], axis=1).astype(BF16)
    w_b = jnp.concatenate([lin[:, :half], gate[:, :half], lin[:, half:], gate[:, half:],
                           w[:, o_gb:o_m], w[:, o_m:]], axis=1).astype(BF16)
    w_f = jnp.pad(w[:, o_f:o_ga], ((0, 0), (0, LANES - N_HEADS))).astype(BF16)
    return w_a, w_b, w_f


def _layer_group(x, mod, rows_per_batch, batch0, wl, layer, depth, kv_prev):
    h = _norm_call(x, mod, wl["norm_g"], rows_per_batch, batch0)
    lf, q, k_all, v_all, kb, vb, ga = _in_a_call(
        h, wl["w_a"], wl["w_f"], wl["b_f"], wl["q_g"], wl["k_g"], layer, depth, kv_prev)
    u, gb, mab = _in_b_call(h, wl["w_b"])
    return lf, q, (k_all, v_all), kb, vb, ga, u, gb, mab


def kernel(x_prompt, x_sample, c_prompt, c_sample, cache_k, cache_v, cache_logf, state_conv, w_ada, b_ada, norm_g, w_in, b_f, q_norm_g, k_norm_g, conv_w, conv_b, conv_ln_g, conv_ln_b, w_oa, w_ob, w_out):
    batch, seq, d = x_prompt.shape
    dec_batch, dec_seq, _ = x_sample.shape
    depth = w_ada.shape[0]
    past = cache_k.shape[2]
    assert d == D_MODEL and seq % 1024 == 0 and (dec_batch * dec_seq) % 1024 == 0
    assert dec_seq >= HALO and seq >= HALO and dec_seq <= LANES

    nb = batch + dec_batch
    nb_pad = -(-nb // 8) * 8
    c_all = jnp.concatenate([c_prompt, c_sample, jnp.zeros((nb_pad - nb, d), F32)], axis=0)
    mod_all = _ada_call(c_all, w_ada, b_ada)

    xp = x_prompt.reshape(batch * seq, d)
    xs = x_sample.reshape(dec_batch * dec_seq, d)
    cache_k2 = cache_k.reshape(depth, dec_batch, past * N_HEADS, HEAD_DIM)
    cache_v2 = cache_v.reshape(depth, dec_batch, past * N_HEADS, HEAD_DIM)
    cache_lf_t = jnp.swapaxes(cache_logf, 2, 3)

    outs = {name: [] for name in ("fp", "cp", "fs", "cs")}
    kv_p = kv_s = None
    for l in range(depth):
        w_a, w_b, w_f = _prep_w_in(w_in[l])
        wl = dict(
            norm_g=norm_g[l].reshape(1, d), w_a=w_a, w_b=w_b, w_f=w_f,
            b_f=jnp.pad(b_f[l], (0, LANES - N_HEADS)).reshape(1, LANES),
            q_g=q_norm_g[l].reshape(1, HEAD_DIM), k_g=k_norm_g[l].reshape(1, HEAD_DIM),
        )
        cw, cb = conv_w[l], conv_b[l].reshape(1, D_CONV)
        lg, lb = conv_ln_g[l].reshape(1, D_CONV), conv_ln_b[l].reshape(1, D_CONV)
        woa, wob, wout = w_oa[l].astype(BF16), w_ob[l].astype(BF16), w_out[l].astype(BF16)
        mod = mod_all[l]

        lf, q, kv_p, kb, vb, ga, u, gb, mab = _layer_group(xp, mod, seq, 0, wl, l, depth, kv_p)
        f_col, f_row = _cum_p_call(lf, batch, seq)
        ya = _attn_p_call(q, kb, vb, f_col, f_row, ga, batch, seq)
        yb = _conv_call(u, u, gb, cw, cb, lg, lb, 256, seq // 256, True)
        xp = _out_call(ya, yb, mab, xp, mod, woa, wob, wout, seq, 0)
        outs["fp"].append(lf[:, :N_HEADS].reshape(batch, seq, N_HEADS))
        outs["cp"].append(u.reshape(batch, seq, D_CONV)[:, seq - HALO:, :])

        lf, q, kv_s, kb, vb, ga, u, gb, mab = _layer_group(xs, mod, dec_seq, batch, wl, l, depth, kv_s)
        g_row, fn_col, fn_row = _cum_s_call(cache_lf_t[l], lf, dec_batch, past, dec_seq)
        ya = _attn_s_call(q, cache_k2, cache_v2, g_row, fn_col, fn_row, kb, vb, ga,
                          l, dec_batch, past, dec_seq)
        yb = _conv_call(u, state_conv[l], gb, cw, cb, lg, lb, dec_seq, 1, False)
        xs = _out_call(ya, yb, mab, xs, mod, woa, wob, wout, dec_seq, batch)
        outs["fs"].append(lf[:, :N_HEADS].reshape(dec_batch, dec_seq, N_HEADS))
        outs["cs"].append(u.reshape(dec_batch, dec_seq, D_CONV)[:, dec_seq - HALO:, :])

    st = {name: jnp.stack(vals) for name, vals in outs.items()}
    cache_p = (depth, batch, seq, N_HEADS, HEAD_DIM)
    cache_s = (depth, dec_batch, dec_seq, N_HEADS, HEAD_DIM)
    return (xp.reshape(batch, seq, d), xs.reshape(dec_batch, dec_seq, d),
            kv_p[0].reshape(cache_p), kv_p[1].reshape(cache_p), st["fp"], st["cp"],
            kv_s[0].reshape(cache_s), kv_s[1].reshape(cache_s), st["fs"], st["cs"])
```

```python
import functools

import jax
import jax.numpy as jnp
from jax import lax
from jax.experimental import pallas as pl
from jax.experimental.pallas import tpu as pltpu

D_MODEL = 2048
N_HEADS = 8
HEAD_DIM = 128
D_ATTN = N_HEADS * HEAD_DIM
D_CONV = 1024
CONV_WIDTH = 31
HALO = CONV_WIDTH - 1
EPS = 1e-6
NEG = -1e30
LOG2E = 1.4426950408889634

LANES = 128
HALO_PAD = 32
CUM_CHUNK = 256
MIB = 1024 * 1024

BF16 = jnp.bfloat16
F32 = jnp.float32


def _params(sem, vmem_mib):
    return pltpu.CompilerParams(dimension_semantics=sem, vmem_limit_bytes=vmem_mib * MIB)


def _const_spec(shape):
    nd = len(shape)
    return pl.BlockSpec(shape, lambda *_: (0,) * nd, pipeline_mode=pl.Buffered(1))


def _sigmoid(x):
    return 1.0 / (1.0 + jnp.exp(-x))


def _silu(x):
    return x * _sigmoid(x)


def _log_sigmoid(x):
    return jnp.minimum(x, 0.0) - jnp.log1p(jnp.exp(-jnp.abs(x)))


def _nt_dot(a, b):
    return lax.dot_general(a, b, (((1,), (1,)), ((), ())), preferred_element_type=F32)


def _dot(a, b):
    return jnp.dot(a, b, preferred_element_type=F32)


def _split3(x):
    hi = x.astype(BF16)
    r1 = x - hi.astype(F32)
    mid = r1.astype(BF16)
    lo = (r1 - mid.astype(F32)).astype(BF16)
    return hi, mid, lo


def _ada_kernel(c_ref, w_ref, b_ref, o_ref):
    s = _silu(c_ref[...]).astype(BF16)
    o_ref[...] = _dot(s, w_ref[...].astype(BF16)) + b_ref[...]


def _ada_call(c_all, w_ada, b_ada):
    depth, d, e = w_ada.shape
    nb = c_all.shape[0]
    tn = 1024
    return pl.pallas_call(
        _ada_kernel,
        grid=(depth, e // tn),
        in_specs=[
            pl.BlockSpec((nb, d), lambda l, n: (0, 0)),
            pl.BlockSpec((None, d, tn), lambda l, n: (l, 0, n)),
            pl.BlockSpec((None, 1, tn), lambda l, n: (l, 0, n)),
        ],
        out_specs=pl.BlockSpec((None, nb, tn), lambda l, n: (l, 0, n)),
        out_shape=jax.ShapeDtypeStruct((depth, nb, e), F32),
        compiler_params=_params(("parallel", "parallel"), 32),
        name="ada_mod",
    )(c_all, w_ada, b_ada.reshape(depth, 1, e))


def _batch_of(row, rows_per_batch, batch0):
    return batch0 + lax.div(row, rows_per_batch)


def _norm_kernel(x_ref, mod_ref, g_ref, h_ref, *, tm, rows_per_batch, batch0):
    m = pl.program_id(0)
    sub = min(tm, rows_per_batch)
    for i in range(tm // sub):
        b = _batch_of(m * tm + i * sub, rows_per_batch, batch0)
        shift = mod_ref[pl.ds(b, 1), 0:D_MODEL]
        scale = mod_ref[pl.ds(b, 1), D_MODEL:2 * D_MODEL]
        x = x_ref[i * sub:(i + 1) * sub, :]
        ms = jnp.mean(x * x, axis=-1, keepdims=True)
        y = x * lax.rsqrt(ms + EPS) * g_ref[...]
        h_ref[i * sub:(i + 1) * sub, :] = (y * (1.0 + scale) + shift).astype(BF16)


def _norm_call(x, mod, g, rows_per_batch, batch0):
    rows = x.shape[0]
    tm = 512
    kern = functools.partial(_norm_kernel, tm=tm, rows_per_batch=rows_per_batch, batch0=batch0)
    return pl.pallas_call(
        kern,
        grid=(rows // tm,),
        in_specs=[
            pl.BlockSpec((tm, D_MODEL), lambda m: (m, 0)),
            _const_spec(mod.shape),
            _const_spec(g.shape),
        ],
        out_specs=pl.BlockSpec((tm, D_MODEL), lambda m: (m, 0)),
        out_shape=jax.ShapeDtypeStruct((rows, D_MODEL), BF16),
        compiler_params=_params(("parallel",), 32),
        name="norm_mod",
    )(x, mod, g)


def _head_rmsnorm(a, g):
    ms = jnp.mean(a * a, axis=-1, keepdims=True)
    return a * lax.rsqrt(ms + EPS) * g


def _store_heads(y, hd, tm, cache_ref, bf_ref):
    cache_ref[pl.ds(hd, tm, stride=N_HEADS), :] = y
    bf_ref[:, hd * HEAD_DIM:(hd + 1) * HEAD_DIM] = y.astype(BF16)


def _in_a_kernel(*refs, tm, n_alias):
    (h_ref, w_ref, wf_ref, bf_ref, qg_ref, kg_ref) = refs[:6]
    (lf_ref, q_ref, k_ref, v_ref, kb_ref, vb_ref, ga_ref) = refs[6 + n_alias:]
    n = pl.program_id(1)

    def head(acc, hd):
        return acc[:, hd * HEAD_DIM:(hd + 1) * HEAD_DIM]

    @pl.when(n == 0)
    def _():
        f = _dot(h_ref[...], wf_ref[...]) + bf_ref[...]
        lf_ref[...] = _log_sigmoid(f)
        acc = _dot(h_ref[...], w_ref[...])
        for hd in range(N_HEADS):
            y = _head_rmsnorm(head(acc, hd), qg_ref[...]) * (HEAD_DIM ** -0.5 * LOG2E)
            q_ref[:, hd * HEAD_DIM:(hd + 1) * HEAD_DIM] = y.astype(BF16)

    @pl.when(n == 1)
    def _():
        acc = _dot(h_ref[...], w_ref[...])
        for hd in range(N_HEADS):
            _store_heads(_head_rmsnorm(head(acc, hd), kg_ref[...]), hd, tm, k_ref, kb_ref)

    @pl.when(n == 2)
    def _():
        acc = _dot(h_ref[...], w_ref[...])
        for hd in range(N_HEADS):
            _store_heads(head(acc, hd), hd, tm, v_ref, vb_ref)

    @pl.when(n == 3)
    def _():
        ga_ref[...] = _silu(_dot(h_ref[...], w_ref[...])).astype(BF16)


def _in_a_call(h, w_a, w_f, b_f, q_g, k_g, layer, depth, kv_prev):
    rows = h.shape[0]
    tm, tn = 1024, 1024
    row_blk = lambda width: pl.BlockSpec((tm, width), lambda m, n: (m, 0))
    cache_blk = pl.BlockSpec((None, tm * N_HEADS, HEAD_DIM), lambda m, n: (layer, m, 0))
    cache_shape = jax.ShapeDtypeStruct((depth, rows * N_HEADS, HEAD_DIM), F32)
    alias_in = [] if kv_prev is None else list(kv_prev)
    n_in = 6
    return pl.pallas_call(
        functools.partial(_in_a_kernel, tm=tm, n_alias=len(alias_in)),
        grid=(rows // tm, 4),
        in_specs=[
            row_blk(D_MODEL),
            pl.BlockSpec((D_MODEL, tn), lambda m, n: (0, n)),
            _const_spec(w_f.shape),
            _const_spec(b_f.shape),
            _const_spec(q_g.shape),
            _const_spec(k_g.shape),
        ] + [pl.BlockSpec(memory_space=pl.ANY)] * len(alias_in),
        out_specs=[row_blk(LANES), row_blk(D_ATTN), cache_blk, cache_blk,
                   row_blk(D_ATTN), row_blk(D_ATTN), row_blk(D_ATTN)],
        out_shape=[
            jax.ShapeDtypeStruct((rows, LANES), F32),
            jax.ShapeDtypeStruct((rows, D_ATTN), BF16),
            cache_shape,
            cache_shape,
            jax.ShapeDtypeStruct((rows, D_ATTN), BF16),
            jax.ShapeDtypeStruct((rows, D_ATTN), BF16),
            jax.ShapeDtypeStruct((rows, D_ATTN), BF16),
        ],
        input_output_aliases={n_in + i: 2 + i for i in range(len(alias_in))},
        compiler_params=_params(("parallel", "arbitrary"), 58),
        name="in_proj_a",
    )(h, w_a, w_f, b_f, q_g, k_g, *alias_in)


def _in_b_kernel(h_ref, w_ref, u_ref, gb_ref, mab_ref):
    n = pl.program_id(1)

    @pl.when(n < 2)
    def _():
        acc = _dot(h_ref[...], w_ref[...])
        half = D_CONV // 2
        u_ref[...] = acc[:, :half] * _sigmoid(acc[:, half:])

    @pl.when(n == 2)
    def _():
        gb_ref[...] = _silu(_dot(h_ref[...], w_ref[...])).astype(BF16)

    @pl.when(n > 2)
    def _():
        mab_ref[...] = _sigmoid(_dot(h_ref[...], w_ref[...])).astype(BF16)


def _in_b_call(h, w_b):
    rows = h.shape[0]
    tm, tn = 1024, 1024
    return pl.pallas_call(
        _in_b_kernel,
        grid=(rows // tm, 7),
        in_specs=[
            pl.BlockSpec((tm, D_MODEL), lambda m, n: (m, 0)),
            pl.BlockSpec((D_MODEL, tn), lambda m, n: (0, n)),
        ],
        out_specs=[
            pl.BlockSpec((tm, D_CONV // 2), lambda m, n: (m, jnp.minimum(n, 1))),
            pl.BlockSpec((tm, D_CONV), lambda m, n: (m, 0)),
            pl.BlockSpec((tm, tn), lambda m, n: (m, jnp.clip(n - 3, 0, 3))),
        ],
        out_shape=[
            jax.ShapeDtypeStruct((rows, D_CONV), F32),
            jax.ShapeDtypeStruct((rows, D_CONV), BF16),
            jax.ShapeDtypeStruct((rows, 2 * D_MODEL), BF16),
        ],
        compiler_params=_params(("parallel", "arbitrary"), 44),
        name="in_proj_b",
    )(h, w_b)


def _tri(n, upper):
    r = lax.broadcasted_iota(jnp.int32, (n, n), 0)
    c = lax.broadcasted_iota(jnp.int32, (n, n), 1)
    keep = (r <= c) if upper else (c <= r)
    return jnp.where(keep, 1.0, 0.0).astype(BF16)


def _cum_p_kernel(lf_ref, fc_ref, fr_ref, *, seq):
    tril = _tri(CUM_CHUNK, upper=False)
    carry = jnp.zeros((1, LANES), F32)
    for i in range(seq // CUM_CHUNK):
        sl = slice(i * CUM_CHUNK, (i + 1) * CUM_CHUNK)
        hi, mid, lo = _split3(lf_ref[sl, :])
        y = _dot(tril, hi) + _dot(tril, mid) + _dot(tril, lo) + carry
        carry = y[CUM_CHUNK - 1:CUM_CHUNK, :]
        y2 = y * LOG2E
        fc_ref[sl, :] = y2
        fr_ref[:, sl] = y2.T[0:N_HEADS, :]


def _cum_p_call(lf, batch, seq):
    kern = functools.partial(_cum_p_kernel, seq=seq)
    return pl.pallas_call(
        kern,
        grid=(batch,),
        in_specs=[pl.BlockSpec((seq, LANES), lambda b: (b, 0))],
        out_specs=[
            pl.BlockSpec((seq, LANES), lambda b: (b, 0)),
            pl.BlockSpec((None, N_HEADS, seq), lambda b: (b, 0, 0)),
        ],
        out_shape=[
            jax.ShapeDtypeStruct((batch * seq, LANES), F32),
            jax.ShapeDtypeStruct((batch, N_HEADS, seq), F32),
        ],
        compiler_params=_params(("parallel",), 32),
        name="cum_prompt",
    )(lf)


def _cum_s_kernel(clf_ref, lfn_ref, g_ref, fnc_ref, fnr_ref, f_sc, *, past, new):
    triu = _tri(CUM_CHUNK, upper=True)
    carry = jnp.zeros((N_HEADS, 1), F32)
    for i in range(past // CUM_CHUNK):
        sl = slice(i * CUM_CHUNK, (i + 1) * CUM_CHUNK)
        hi, mid, lo = _split3(clf_ref[:, sl])
        y = _dot(hi, triu) + _dot(mid, triu) + _dot(lo, triu) + carry
        carry = y[:, CUM_CHUNK - 1:CUM_CHUNK]
        f_sc[:, sl] = y
    g_ref[...] = (carry - f_sc[...]) * LOG2E
    tril = _tri(LANES, upper=False)
    xn = jnp.concatenate([lfn_ref[...], jnp.zeros((LANES - new, LANES), F32)], axis=0)
    hi, mid, lo = _split3(xn)
    yn = (_dot(tril, hi) + _dot(tril, mid) + _dot(tril, lo)) * LOG2E
    fnc_ref[...] = yn[0:new, :]
    fnr_ref[...] = yn.T[0:N_HEADS, :]


def _cum_s_call(clf_t, lf_new, batch, past, new):
    kern = functools.partial(_cum_s_kernel, past=past, new=new)
    return pl.pallas_call(
        kern,
        grid=(batch,),
        in_specs=[
            pl.BlockSpec((None, N_HEADS, past), lambda b: (b, 0, 0)),
            pl.BlockSpec((new, LANES), lambda b: (b, 0)),
        ],
        out_specs=[
            pl.BlockSpec((None, N_HEADS, past), lambda b: (b, 0, 0)),
            pl.BlockSpec((new, LANES), lambda b: (b, 0)),
            pl.BlockSpec((None, N_HEADS, LANES), lambda b: (b, 0, 0)),
        ],
        out_shape=[
            jax.ShapeDtypeStruct((batch, N_HEADS, past), F32),
            jax.ShapeDtypeStruct((batch * new, LANES), F32),
            jax.ShapeDtypeStruct((batch, N_HEADS, LANES), F32),
        ],
        scratch_shapes=[pltpu.VMEM((N_HEADS, past), F32)],
        compiler_params=_params(("parallel",), 32),
        name="cum_sample",
    )(clf_t, lf_new)


def _softmax_parts(carry, t, fq):
    m, l, acc = carry
    m_new = jnp.maximum(m, jnp.max(t, axis=-1, keepdims=True) + fq)
    alpha = jnp.exp2(m - m_new)
    p = jnp.exp2(t + (fq - m_new))
    return m_new, alpha * l + jnp.sum(p, axis=-1, keepdims=True), alpha * acc, p.astype(BF16)


def _softmax_step(carry, t, fq, v):
    m_new, l, acc, p = _softmax_parts(carry, t, fq)
    return m_new, l, acc + _dot(p, v)


def _causal(s):
    row = lax.broadcasted_iota(jnp.int32, s.shape, 0)
    col = lax.broadcasted_iota(jnp.int32, s.shape, 1)
    return jnp.where(col <= row, s, NEG)


def _attn_p_kernel(q_ref, k_ref, v_ref, fc_ref, fr_ref, ga_ref, o_ref, *, seq, tq):
    hd = pl.program_id(1)
    nq = seq // tq
    lane = lax.broadcasted_iota(jnp.int32, (seq, LANES), 1)
    fq_all = jnp.sum(jnp.where(lane == hd, fc_ref[...], 0.0), axis=-1, keepdims=True)
    fr = fr_ref[pl.ds(hd, 1), :]
    blk = lambda i: slice(i * tq, (i + 1) * tq)

    def scores(i, j):
        t = _nt_dot(q_ref[blk(i), :], k_ref[blk(j), :]) - fr[:, blk(j)]
        return _causal(t) if i == j else t

    pairs = [(i, j) for i in range(nq) for j in range(i + 1)]
    t_next = scores(*pairs[0])
    carry = None
    for n, (i, j) in enumerate(pairs):
        t = t_next
        if n + 1 < len(pairs):
            t_next = scores(*pairs[n + 1])
        if j == 0:
            carry = (jnp.full((tq, 1), NEG, F32), jnp.zeros((tq, 1), F32), jnp.zeros((tq, HEAD_DIM), F32))
        carry = _softmax_step(carry, t, fq_all[blk(i), :], v_ref[blk(j), :])
        if j == i:
            _, l, acc = carry
            o_ref[blk(i), :] = (acc / l * ga_ref[blk(i), :].astype(F32)).astype(BF16)


def _attn_p_call(q, k, v, f_col, f_row, ga, batch, seq):
    tq = 512
    head = pl.BlockSpec((seq, HEAD_DIM), lambda b, h: (b, h))
    return pl.pallas_call(
        functools.partial(_attn_p_kernel, seq=seq, tq=tq),
        grid=(batch, N_HEADS),
        in_specs=[
            head, head, head,
            pl.BlockSpec((seq, LANES), lambda b, h: (b, 0)),
            pl.BlockSpec((None, N_HEADS, seq), lambda b, h: (b, 0, 0)),
            head,
        ],
        out_specs=head,
        out_shape=jax.ShapeDtypeStruct((batch * seq, D_ATTN), BF16),
        compiler_params=_params(("parallel", "parallel"), 32),
        name="attn_prompt",
    )(q, k, v, f_col, f_row, ga)


def _attn_s_kernel(q_ref, ck_ref, cv_ref, g_ref, fnc_ref, fnr_ref, kn_ref, vn_ref, ga_ref,
                   o_ref, m_sc, l_sc, acc_sc, *, new, tk):
    j = pl.program_id(1)
    last = pl.num_programs(1) - 1

    @pl.when(j == 0)
    def _():
        m_sc[...] = jnp.full(m_sc.shape, NEG, F32)
        l_sc[...] = jnp.zeros(l_sc.shape, F32)
        acc_sc[...] = jnp.zeros(acc_sc.shape, F32)

    heads = range(N_HEADS)
    sls = [slice(hd * HEAD_DIM, (hd + 1) * HEAD_DIM) for hd in heads]
    state = [(m_sc[hd][:, 0:1], l_sc[hd][:, 0:1], acc_sc[:, sls[hd]]) for hd in heads]
    fq = [fnc_ref[:, hd:hd + 1] for hd in heads]

    ts = [_nt_dot(q_ref[:, sls[hd]], ck_ref[pl.ds(hd, tk, stride=N_HEADS), :].astype(BF16)) + g_ref[hd:hd + 1, :]
          for hd in heads]
    parts = [_softmax_parts(state[hd], ts[hd], fq[hd]) for hd in heads]
    out = []
    for hd in heads:
        m, l, acc, p = parts[hd]
        out.append((m, l, acc + _dot(p, cv_ref[pl.ds(hd, tk, stride=N_HEADS), :].astype(BF16))))
    for hd in heads:
        m, l, acc = out[hd]
        m_sc[hd] = jnp.broadcast_to(m, (new, LANES))
        l_sc[hd] = jnp.broadcast_to(l, (new, LANES))
        acc_sc[:, sls[hd]] = acc

    @pl.when(j == last)
    def _():
        pad = jnp.zeros((LANES - new, HEAD_DIM), BF16)
        ts = [_causal(_nt_dot(q_ref[:, sls[hd]], jnp.concatenate([kn_ref[:, sls[hd]], pad], axis=0))
                      - fnr_ref[hd:hd + 1, :]) for hd in heads]
        parts = [_softmax_parts(out[hd], ts[hd], fq[hd]) for hd in heads]
        for hd in heads:
            _, l, acc, p = parts[hd]
            acc = acc + _dot(p, jnp.concatenate([vn_ref[:, sls[hd]], pad], axis=0))
            o_ref[:, sls[hd]] = (acc / l * ga_ref[:, sls[hd]].astype(F32)).astype(BF16)


def _attn_s_call(q, cache_k, cache_v, g_row, fn_col, fn_row, k_new, v_new, ga, layer, batch, past, new):
    tk = 2048
    row = lambda width: pl.BlockSpec((new, width), lambda b, j: (b, 0))
    cache = pl.BlockSpec((None, None, tk * N_HEADS, HEAD_DIM), lambda b, j: (layer, b, j, 0))
    return pl.pallas_call(
        functools.partial(_attn_s_kernel, new=new, tk=tk),
        grid=(batch, past // tk),
        in_specs=[
            row(D_ATTN), cache, cache,
            pl.BlockSpec((None, N_HEADS, tk), lambda b, j: (b, 0, j)),
            row(LANES),
            pl.BlockSpec((None, N_HEADS, LANES), lambda b, j: (b, 0, 0)),
            row(D_ATTN), row(D_ATTN), row(D_ATTN),
        ],
        out_specs=row(D_ATTN),
        out_shape=jax.ShapeDtypeStruct((batch * new, D_ATTN), BF16),
        scratch_shapes=[
            pltpu.VMEM((N_HEADS, new, LANES), F32),
            pltpu.VMEM((N_HEADS, new, LANES), F32),
            pltpu.VMEM((new, D_ATTN), F32),
        ],
        compiler_params=_params(("parallel", "arbitrary"), 52),
        name="attn_sample",
    )(q, cache_k, cache_v, g_row, fn_col, fn_row, k_new, v_new, ga)


CONV_ROWS = 64


def _conv_kernel(u_ref, halo_ref, gb_ref, w_ref, cb_ref, lg_ref, lb_ref, o_ref, ext_sc, cv_sc,
                 *, tt, tiles_per_batch, halo_from_u):
    nc = D_CONV // LANES
    chunk = lambda c: slice(c * LANES, (c + 1) * LANES)
    for c in range(nc):
        ext_sc[c, HALO_PAD:HALO_PAD + tt, :] = u_ref[:, chunk(c)]
    if halo_from_u:
        first = lax.rem(pl.program_id(0), tiles_per_batch) == 0

        @pl.when(first)
        def _():
            ext_sc[:, 0:HALO_PAD, :] = jnp.zeros((nc, HALO_PAD, LANES), F32)

        @pl.when(jnp.logical_not(first))
        def _():
            for c in range(nc):
                ext_sc[c, 0:HALO_PAD, :] = halo_ref[:, chunk(c)]
    else:
        for c in range(nc):
            ext_sc[c, HALO_PAD - HALO:HALO_PAD, :] = halo_ref[:, chunk(c)]

    base = HALO_PAD - HALO
    for c in range(nc):
        w_taps = w_ref[:, chunk(c)]
        bias = cb_ref[:, chunk(c)]

        def rows_body(r, carry, c=c, w_taps=w_taps, bias=bias):
            r0 = pl.multiple_of(r * CONV_ROWS, CONV_ROWS)
            acc = jnp.zeros((CONV_ROWS, LANES), F32)
            for j in range(CONV_WIDTH):
                acc = acc + w_taps[j:j + 1, :] * ext_sc[c, pl.ds(r0 + (base + j), CONV_ROWS, stride=1), :]
            cv_sc[pl.ds(r0, CONV_ROWS), chunk(c)] = acc + bias
            return carry

        lax.fori_loop(0, tt // CONV_ROWS, rows_body, 0)

    cv = cv_sc[...]
    mu = jnp.mean(cv, axis=-1, keepdims=True)
    d = cv - mu
    var = jnp.mean(d * d, axis=-1, keepdims=True)
    y = d * lax.rsqrt(var + EPS) * lg_ref[...] + lb_ref[...]
    o_ref[...] = (_silu(y) * gb_ref[...].astype(F32)).astype(BF16)


def _conv_call(u, halo, gb, conv_w, conv_b, ln_g, ln_b, tt, tiles_per_batch, halo_from_u):
    rows = u.shape[0]
    tile = pl.BlockSpec((tt, D_CONV), lambda t: (t, 0))
    if halo_from_u:
        per = tt // HALO_PAD
        halo_spec = pl.BlockSpec((HALO_PAD, D_CONV), lambda t: (jnp.maximum(t * per - 1, 0), 0))
    else:
        halo_spec = pl.BlockSpec((None, HALO, D_CONV), lambda t: (t, 0, 0))
    kern = functools.partial(_conv_kernel, tt=tt, tiles_per_batch=tiles_per_batch, halo_from_u=halo_from_u)
    return pl.pallas_call(
        kern,
        grid=(rows // tt,),
        in_specs=[
            tile, halo_spec, tile,
            _const_spec(conv_w.shape), _const_spec(conv_b.shape),
            _const_spec(ln_g.shape), _const_spec(ln_b.shape),
        ],
        out_specs=tile,
        out_shape=jax.ShapeDtypeStruct((rows, D_CONV), BF16),
        scratch_shapes=[
            pltpu.VMEM((D_CONV // LANES, HALO_PAD + tt, LANES), F32),
            pltpu.VMEM((tt, D_CONV), F32),
        ],
        compiler_params=_params(("parallel",), 32),
        name="conv_module",
    )(u, halo, gb, conv_w, conv_b, ln_g, ln_b)


OUT_COLS = 512


def _out_kernel(ya_ref, yb_ref, mab_ref, x_ref, mod_ref, woa_ref, wob_ref, wout_ref, o_ref, mg_sc,
                *, tm, rows_per_batch, batch0):
    m = pl.program_id(0)
    ya = ya_ref[...]
    yb = yb_ref[...]
    for c in range(D_MODEL // OUT_COLS):
        cs = slice(c * OUT_COLS, (c + 1) * OUT_COLS)
        cs_b = slice(D_MODEL + c * OUT_COLS, D_MODEL + (c + 1) * OUT_COLS)
        pa = _dot(ya, woa_ref[:, cs])
        pb = _dot(yb, wob_ref[:, cs])
        merged = mab_ref[:, cs].astype(F32) * pa + mab_ref[:, cs_b].astype(F32) * pb
        mg_sc[:, cs] = merged.astype(BF16)
    out = _dot(mg_sc[...], wout_ref[...])
    sub = min(tm, rows_per_batch)
    for i in range(tm // sub):
        b = _batch_of(m * tm + i * sub, rows_per_batch, batch0)
        gate = mod_ref[pl.ds(b, 1), 2 * D_MODEL:3 * D_MODEL]
        rs = slice(i * sub, (i + 1) * sub)
        o_ref[rs, :] = x_ref[rs, :] + gate * out[rs, :]


def _out_call(ya, yb, mab, x, mod, w_oa, w_ob, w_out, rows_per_batch, batch0):
    rows = x.shape[0]
    tm = 512
    row_blk = lambda width: pl.BlockSpec((tm, width), lambda m: (m, 0))
    kern = functools.partial(_out_kernel, tm=tm, rows_per_batch=rows_per_batch, batch0=batch0)
    return pl.pallas_call(
        kern,
        grid=(rows // tm,),
        in_specs=[
            row_blk(D_ATTN), row_blk(D_CONV), row_blk(2 * D_MODEL), row_blk(D_MODEL),
            _const_spec(mod.shape), _const_spec(w_oa.shape), _const_spec(w_ob.shape), _const_spec(w_out.shape),
        ],
        out_specs=row_blk(D_MODEL),
        out_shape=jax.ShapeDtypeStruct((rows, D_MODEL), F32),
        scratch_shapes=[pltpu.VMEM((tm, D_MODEL), BF16)],
        compiler_params=_params(("parallel",), 56),
        name="out_proj",
    )(ya, yb, mab, x, mod, w_oa, w_ob, w_out)


def _prep_w_in(w):
    o_f = 3 * D_ATTN
    o_ga = o_f + N_HEADS
    o_glu = o_ga + D_ATTN
    o_gb = o_glu + 2 * D_CONV
    o_m = o_gb + D_CONV
    half = D_CONV // 2
    lin = w[:, o_glu:o_glu + D_CONV]
    gate = w[:, o_glu + D_CONV:o_gb]
    w_a = jnp.concatenate([w[:, :o_f], w[:, o_ga:o_glu]], axis=1).astype(BF16)
    w_b = jnp.concatenate([lin[:, :half], gate[:, :half], lin[:, half:], gate[:, half:],
                           w[:, o_gb:o_m], w[:, o_m:]], axis=1).astype(BF16)
    w_f = jnp.pad(w[:, o_f:o_ga], ((0, 0), (0, LANES - N_HEADS))).astype(BF16)
    return w_a, w_b, w_f


def _layer_group(x, mod, rows_per_batch, batch0, wl, layer, depth, kv_prev):
    h = _norm_call(x, mod, wl["norm_g"], rows_per_batch, batch0)
    lf, q, k_all, v_all, kb, vb, ga = _in_a_call(
        h, wl["w_a"], wl["w_f"], wl["b_f"], wl["q_g"], wl["k_g"], layer, depth, kv_prev)
    u, gb, mab = _in_b_call(h, wl["w_b"])
    return lf, q, (k_all, v_all), kb, vb, ga, u, gb, mab


def kernel(x_prompt, x_sample, c_prompt, c_sample, cache_k, cache_v, cache_logf, state_conv, w_ada, b_ada, norm_g, w_in, b_f, q_norm_g, k_norm_g, conv_w, conv_b, conv_ln_g, conv_ln_b, w_oa, w_ob, w_out):
    batch, seq, d = x_prompt.shape
    dec_batch, dec_seq, _ = x_sample.shape
    depth = w_ada.shape[0]
    past = cache_k.shape[2]
    assert d == D_MODEL and seq % 1024 == 0 and (dec_batch * dec_seq) % 1024 == 0
    assert dec_seq >= HALO and seq >= HALO and dec_seq <= LANES

    nb = batch + dec_batch
    nb_pad = -(-nb // 8) * 8
    c_all = jnp.concatenate([c_prompt, c_sample, jnp.zeros((nb_pad - nb, d), F32)], axis=0)
    mod_all = _ada_call(c_all, w_ada, b_ada)

    xp = x_prompt.reshape(batch * seq, d)
    xs = x_sample.reshape(dec_batch * dec_seq, d)
    cache_k2 = cache_k.reshape(depth, dec_batch, past * N_HEADS, HEAD_DIM)
    cache_v2 = cache_v.reshape(depth, dec_batch, past * N_HEADS, HEAD_DIM)
    cache_lf_t = jnp.swapaxes(cache_logf, 2, 3)

    outs = {name: [] for name in ("fp", "cp", "fs", "cs")}
    kv_p = kv_s = None
    for l in range(depth):
        w_a, w_b, w_f = _prep_w_in(w_in[l])
        wl = dict(
            norm_g=norm_g[l].reshape(1, d), w_a=w_a, w_b=w_b, w_f=w_f,
            b_f=jnp.pad(b_f[l], (0, LANES - N_HEADS)).reshape(1, LANES),
            q_g=q_norm_g[l].reshape(1, HEAD_DIM), k_g=k_norm_g[l].reshape(1, HEAD_DIM),
        )
        cw, cb = conv_w[l], conv_b[l].reshape(1, D_CONV)
        lg, lb = conv_ln_g[l].reshape(1, D_CONV), conv_ln_b[l].reshape(1, D_CONV)
        woa, wob, wout = w_oa[l].astype(BF16), w_ob[l].astype(BF16), w_out[l].astype(BF16)
        mod = mod_all[l]

        lf, q, kv_p, kb, vb, ga, u, gb, mab = _layer_group(xp, mod, seq, 0, wl, l, depth, kv_p)
        f_col, f_row = _cum_p_call(lf, batch, seq)
        ya = _attn_p_call(q, kb, vb, f_col, f_row, ga, batch, seq)
        yb = _conv_call(u, u, gb, cw, cb, lg, lb, 256, seq // 256, True)
        xp = _out_call(ya, yb, mab, xp, mod, woa, wob, wout, seq, 0)
        outs["fp"].append(lf[:, :N_HEADS].reshape(batch, seq, N_HEADS))
        outs["cp"].append(u.reshape(batch, seq, D_CONV)[:, seq - HALO:, :])

        lf, q, kv_s, kb, vb, ga, u, gb, mab = _layer_group(xs, mod, dec_seq, batch, wl, l, depth, kv_s)
        g_row, fn_col, fn_row = _cum_s_call(cache_lf_t[l], lf, dec_batch, past, dec_seq)
        ya = _attn_s_call(q, cache_k2, cache_v2, g_row, fn_col, fn_row, kb, vb, ga,
                          l, dec_batch, past, dec_seq)
        yb = _conv_call(u, state_conv[l], gb, cw, cb, lg, lb, dec_seq, 1, False)
        xs = _out_call(ya, yb, mab, xs, mod, woa, wob, wout, dec_seq, batch)
        outs["fs"].append(lf[:, :N_HEADS].reshape(dec_batch, dec_seq, N_HEADS))
        outs["cs"].append(u.reshape(dec_batch, dec_seq, D_CONV)[:, dec_seq - HALO:, :])

    st = {name: jnp.stack(vals) for name, vals in outs.items()}
    cache_p = (depth, batch, seq, N_HEADS, HEAD_DIM)
    cache_s = (depth, dec_batch, dec_seq, N_HEADS, HEAD_DIM)
    return (xp.reshape(batch, seq, d), xs.reshape(dec_batch, dec_seq, d),
            kv_p[0].reshape(cache_p), kv_p[1].reshape(cache_p), st["fp"], st["cp"],
            kv_s[0].reshape(cache_s), kv_s[1].reshape(cache_s), st["fs"], st["cs"])
```

```python
import functools

import jax
import jax.numpy as jnp
from jax import lax
from jax.experimental import pallas as pl
from jax.experimental.pallas import tpu as pltpu

D_MODEL = 2048
N_HEADS = 8
HEAD_DIM = 128
D_ATTN = N_HEADS * HEAD_DIM
D_CONV = 1024
CONV_WIDTH = 31
HALO = CONV_WIDTH - 1
EPS = 1e-6
NEG = -1e30
LOG2E = 1.4426950408889634

LANES = 128
HALO_PAD = 32
CUM_CHUNK = 256
MIB = 1024 * 1024

BF16 = jnp.bfloat16
F32 = jnp.float32


def _params(sem, vmem_mib):
    return pltpu.CompilerParams(dimension_semantics=sem, vmem_limit_bytes=vmem_mib * MIB)


def _layer_spec(shape, layer):
    nd = len(shape) - 1
    return pl.BlockSpec((None,) + tuple(shape[1:]), lambda *_: (layer,) + (0,) * nd, pipeline_mode=pl.Buffered(1))


def _sigmoid(x):
    return 1.0 / (1.0 + jnp.exp(-x))


def _silu(x):
    return x * _sigmoid(x)


def _log_sigmoid(x):
    return jnp.minimum(x, 0.0) - jnp.log1p(jnp.exp(-jnp.abs(x)))


def _nt_dot(a, b):
    return lax.dot_general(a, b, (((1,), (1,)), ((), ())), preferred_element_type=F32)


def _dot(a, b):
    return jnp.dot(a, b, preferred_element_type=F32)


def _split3(x):
    hi = x.astype(BF16)
    r1 = x - hi.astype(F32)
    mid = r1.astype(BF16)
    lo = (r1 - mid.astype(F32)).astype(BF16)
    return hi, mid, lo


def _ada_kernel(c_ref, w_ref, b_ref, o_ref):
    s = _silu(c_ref[...]).astype(BF16)
    o_ref[...] = _dot(s, w_ref[...].astype(BF16)) + b_ref[...]


def _ada_call(c_all, w_ada, b_ada):
    depth, d, e = w_ada.shape
    nb = c_all.shape[0]
    tn = 1024
    return pl.pallas_call(
        _ada_kernel,
        grid=(depth, e // tn),
        in_specs=[
            pl.BlockSpec((nb, d), lambda l, n: (0, 0)),
            pl.BlockSpec((None, d, tn), lambda l, n: (l, 0, n)),
            pl.BlockSpec((None, 1, tn), lambda l, n: (l, 0, n)),
        ],
        out_specs=pl.BlockSpec((None, nb, tn), lambda l, n: (l, 0, n)),
        out_shape=jax.ShapeDtypeStruct((depth, nb, e), F32),
        compiler_params=_params(("parallel", "parallel"), 32),
        name="ada_mod",
    )(c_all, w_ada, b_ada.reshape(depth, 1, e))


def _batch_of(row, rows_per_batch, batch0):
    return batch0 + lax.div(row, rows_per_batch)


def _norm_kernel(x_ref, mod_ref, g_ref, h_ref, *, tm, rows_per_batch, batch0):
    m = pl.program_id(0)
    sub = min(tm, rows_per_batch)
    for i in range(tm // sub):
        b = _batch_of(m * tm + i * sub, rows_per_batch, batch0)
        shift = mod_ref[pl.ds(b, 1), 0:D_MODEL]
        scale = mod_ref[pl.ds(b, 1), D_MODEL:2 * D_MODEL]
        x = x_ref[i * sub:(i + 1) * sub, :]
        ms = jnp.mean(x * x, axis=-1, keepdims=True)
        y = x * lax.rsqrt(ms + EPS) * g_ref[...]
        h_ref[i * sub:(i + 1) * sub, :] = (y * (1.0 + scale) + shift).astype(BF16)


def _norm_call(x, mod, g, layer, rows_per_batch, batch0):
    rows = x.shape[0]
    tm = 512
    kern = functools.partial(_norm_kernel, tm=tm, rows_per_batch=rows_per_batch, batch0=batch0)
    return pl.pallas_call(
        kern,
        grid=(rows // tm,),
        in_specs=[
            pl.BlockSpec((tm, D_MODEL), lambda m: (m, 0)),
            _layer_spec(mod.shape, layer),
            _layer_spec(g.shape, layer),
        ],
        out_specs=pl.BlockSpec((tm, D_MODEL), lambda m: (m, 0)),
        out_shape=jax.ShapeDtypeStruct((rows, D_MODEL), BF16),
        compiler_params=_params(("parallel",), 32),
        name="norm_mod",
    )(x, mod, g)


def _head_rmsnorm(a, g):
    ms = jnp.mean(a * a, axis=-1, keepdims=True)
    return a * lax.rsqrt(ms + EPS) * g


def _store_heads(y, hd, tm, cache_ref, bf_ref):
    cache_ref[pl.ds(hd, tm, stride=N_HEADS), :] = y
    bf_ref[:, hd * HEAD_DIM:(hd + 1) * HEAD_DIM] = y.astype(BF16)


def _in_a_kernel(*refs, tm, n_alias):
    (h_ref, w_ref, wf_ref, bf_ref, qg_ref, kg_ref) = refs[:6]
    (lf_ref, lf8_ref, q_ref, k_ref, v_ref, kb_ref, vb_ref, ga_ref) = refs[6 + n_alias:]
    n = pl.program_id(1)

    def head(acc, hd):
        return acc[:, hd * HEAD_DIM:(hd + 1) * HEAD_DIM]

    @pl.when(n == 0)
    def _():
        f = _dot(h_ref[...], wf_ref[...]) + bf_ref[...]
        lf = _log_sigmoid(f)
        lf_ref[...] = lf
        lf8_ref[...] = lf[:, 0:N_HEADS]
        acc = _dot(h_ref[...], w_ref[...])
        for hd in range(N_HEADS):
            y = _head_rmsnorm(head(acc, hd), qg_ref[...]) * (HEAD_DIM ** -0.5 * LOG2E)
            q_ref[:, hd * HEAD_DIM:(hd + 1) * HEAD_DIM] = y.astype(BF16)

    @pl.when(n == 1)
    def _():
        acc = _dot(h_ref[...], w_ref[...])
        for hd in range(N_HEADS):
            _store_heads(_head_rmsnorm(head(acc, hd), kg_ref[...]), hd, tm, k_ref, kb_ref)

    @pl.when(n == 2)
    def _():
        acc = _dot(h_ref[...], w_ref[...])
        for hd in range(N_HEADS):
            _store_heads(head(acc, hd), hd, tm, v_ref, vb_ref)

    @pl.when(n == 3)
    def _():
        ga_ref[...] = _silu(_dot(h_ref[...], w_ref[...])).astype(BF16)


def _in_a_call(h, w_all, w_f, b_f, q_g, k_g, layer, depth, stacked_prev):
    rows = h.shape[0]
    tm, tn = 1024, 1024
    row_blk = lambda width: pl.BlockSpec((tm, width), lambda m, n: (m, 0))
    cache_blk = pl.BlockSpec((None, tm * N_HEADS, HEAD_DIM), lambda m, n: (layer, m, 0))
    cache_shape = jax.ShapeDtypeStruct((depth, rows * N_HEADS, HEAD_DIM), F32)
    alias_in = [] if stacked_prev is None else list(stacked_prev)
    n_in = 6
    return pl.pallas_call(
        functools.partial(_in_a_kernel, tm=tm, n_alias=len(alias_in)),
        grid=(rows // tm, 4),
        in_specs=[
            row_blk(D_MODEL),
            pl.BlockSpec((None, D_MODEL, tn), lambda m, n: (layer, 0, n)),
            _layer_spec(w_f.shape, layer),
            _layer_spec(b_f.shape, layer),
            _layer_spec(q_g.shape, layer),
            _layer_spec(k_g.shape, layer),
        ] + [pl.BlockSpec(memory_space=pl.ANY)] * len(alias_in),
        out_specs=[row_blk(LANES),
                   pl.BlockSpec((None, tm, N_HEADS), lambda m, n: (layer, m, 0)),
                   row_blk(D_ATTN), cache_blk, cache_blk,
                   row_blk(D_ATTN), row_blk(D_ATTN), row_blk(D_ATTN)],
        out_shape=[
            jax.ShapeDtypeStruct((rows, LANES), F32),
            jax.ShapeDtypeStruct((depth, rows, N_HEADS), F32),
            jax.ShapeDtypeStruct((rows, D_ATTN), BF16),
            cache_shape,
            cache_shape,
            jax.ShapeDtypeStruct((rows, D_ATTN), BF16),
            jax.ShapeDtypeStruct((rows, D_ATTN), BF16),
            jax.ShapeDtypeStruct((rows, D_ATTN), BF16),
        ],
        input_output_aliases={n_in + i: out for i, out in zip(range(len(alias_in)), (1, 3, 4))},
        compiler_params=_params(("parallel", "arbitrary"), 58),
        name="in_proj_a",
    )(h, w_all, w_f, b_f, q_g, k_g, *alias_in)


IN_A_TILES = 4


def _in_b_kernel(h_ref, w_ref, u_ref, gb_ref, mab_ref, lin_sc):
    n = pl.program_id(1)

    @pl.when(n == 0)
    def _():
        lin_sc[...] = _dot(h_ref[...], w_ref[...])

    @pl.when(n == 1)
    def _():
        u_ref[...] = lin_sc[...] * _sigmoid(_dot(h_ref[...], w_ref[...]))

    @pl.when(n == 2)
    def _():
        gb_ref[...] = _silu(_dot(h_ref[...], w_ref[...])).astype(BF16)

    @pl.when(n > 2)
    def _():
        mab_ref[...] = _sigmoid(_dot(h_ref[...], w_ref[...])).astype(BF16)


def _in_b_call(h, w_all, layer):
    rows = h.shape[0]
    tm, tn = 1024, 1024
    return pl.pallas_call(
        _in_b_kernel,
        grid=(rows // tm, 7),
        in_specs=[
            pl.BlockSpec((tm, D_MODEL), lambda m, n: (m, 0)),
            pl.BlockSpec((None, D_MODEL, tn), lambda m, n: (layer, 0, n + IN_A_TILES)),
        ],
        out_specs=[
            pl.BlockSpec((tm, D_CONV), lambda m, n: (m, 0)),
            pl.BlockSpec((tm, D_CONV), lambda m, n: (m, 0)),
            pl.BlockSpec((tm, tn), lambda m, n: (m, jnp.clip(n - 3, 0, 3))),
        ],
        out_shape=[
            jax.ShapeDtypeStruct((rows, D_CONV), F32),
            jax.ShapeDtypeStruct((rows, D_CONV), BF16),
            jax.ShapeDtypeStruct((rows, 2 * D_MODEL), BF16),
        ],
        scratch_shapes=[pltpu.VMEM((tm, D_CONV), F32)],
        compiler_params=_params(("parallel", "arbitrary"), 48),
        name="in_proj_b",
    )(h, w_all)


def _tri(n, upper):
    r = lax.broadcasted_iota(jnp.int32, (n, n), 0)
    c = lax.broadcasted_iota(jnp.int32, (n, n), 1)
    keep = (r <= c) if upper else (c <= r)
    return jnp.where(keep, 1.0, 0.0).astype(BF16)


def _cum_p_kernel(lf_ref, fc_ref, fr_ref, *, seq):
    tril = _tri(CUM_CHUNK, upper=False)
    carry = jnp.zeros((1, LANES), F32)
    for i in range(seq // CUM_CHUNK):
        sl = slice(i * CUM_CHUNK, (i + 1) * CUM_CHUNK)
        hi, mid, lo = _split3(lf_ref[sl, :])
        y = _dot(tril, hi) + _dot(tril, mid) + _dot(tril, lo) + carry
        carry = y[CUM_CHUNK - 1:CUM_CHUNK, :]
        y2 = y * LOG2E
        fc_ref[sl, :] = y2
        fr_ref[:, sl] = y2.T[0:N_HEADS, :]


def _cum_p_call(lf, batch, seq):
    kern = functools.partial(_cum_p_kernel, seq=seq)
    return pl.pallas_call(
        kern,
        grid=(batch,),
        in_specs=[pl.BlockSpec((seq, LANES), lambda b: (b, 0))],
        out_specs=[
            pl.BlockSpec((seq, LANES), lambda b: (b, 0)),
            pl.BlockSpec((None, N_HEADS, seq), lambda b: (b, 0, 0)),
        ],
        out_shape=[
            jax.ShapeDtypeStruct((batch * seq, LANES), F32),
            jax.ShapeDtypeStruct((batch, N_HEADS, seq), F32),
        ],
        compiler_params=_params(("parallel",), 32),
        name="cum_prompt",
    )(lf)


def _cum_s_kernel(clf_ref, lfn_ref, g_ref, fnc_ref, fnr_ref, f_sc, *, past, new):
    triu = _tri(CUM_CHUNK, upper=True)
    carry = jnp.zeros((N_HEADS, 1), F32)
    for i in range(past // CUM_CHUNK):
        sl = slice(i * CUM_CHUNK, (i + 1) * CUM_CHUNK)
        hi, mid, lo = _split3(clf_ref[:, sl])
        y = _dot(hi, triu) + _dot(mid, triu) + _dot(lo, triu) + carry
        carry = y[:, CUM_CHUNK - 1:CUM_CHUNK]
        f_sc[:, sl] = y
    g_ref[...] = (carry - f_sc[...]) * LOG2E
    tril = _tri(LANES, upper=False)
    xn = jnp.concatenate([lfn_ref[...], jnp.zeros((LANES - new, LANES), F32)], axis=0)
    hi, mid, lo = _split3(xn)
    yn = (_dot(tril, hi) + _dot(tril, mid) + _dot(tril, lo)) * LOG2E
    fnc_ref[...] = yn[0:new, :]
    fnr_ref[...] = yn.T[0:N_HEADS, :]


def _cum_s_call(clf_t, lf_new, layer, batch, past, new):
    kern = functools.partial(_cum_s_kernel, past=past, new=new)
    return pl.pallas_call(
        kern,
        grid=(batch,),
        in_specs=[
            pl.BlockSpec((None, None, N_HEADS, past), lambda b: (layer, b, 0, 0)),
            pl.BlockSpec((new, LANES), lambda b: (b, 0)),
        ],
        out_specs=[
            pl.BlockSpec((None, N_HEADS, past), lambda b: (b, 0, 0)),
            pl.BlockSpec((new, LANES), lambda b: (b, 0)),
            pl.BlockSpec((None, N_HEADS, LANES), lambda b: (b, 0, 0)),
        ],
        out_shape=[
            jax.ShapeDtypeStruct((batch, N_HEADS, past), F32),
            jax.ShapeDtypeStruct((batch * new, LANES), F32),
            jax.ShapeDtypeStruct((batch, N_HEADS, LANES), F32),
        ],
        scratch_shapes=[pltpu.VMEM((N_HEADS, past), F32)],
        compiler_params=_params(("parallel",), 32),
        name="cum_sample",
    )(clf_t, lf_new)


def _softmax_parts(carry, t, fq):
    m, l, acc = carry
    m_new = jnp.maximum(m, jnp.max(t, axis=-1, keepdims=True) + fq)
    alpha = jnp.exp2(m - m_new)
    p = jnp.exp2(t + (fq - m_new))
    return m_new, alpha * l + jnp.sum(p, axis=-1, keepdims=True), alpha * acc, p.astype(BF16)


def _softmax_step(carry, t, fq, v):
    m_new, l, acc, p = _softmax_parts(carry, t, fq)
    return m_new, l, acc + _dot(p, v)


def _causal(s):
    row = lax.broadcasted_iota(jnp.int32, s.shape, 0)
    col = lax.broadcasted_iota(jnp.int32, s.shape, 1)
    return jnp.where(col <= row, s, NEG)


def _attn_p_kernel(q_ref, k_ref, v_ref, fc_ref, fr_ref, ga_ref, o_ref, *, seq, tq):
    hd = pl.program_id(1)
    nq = seq // tq
    lane = lax.broadcasted_iota(jnp.int32, (seq, LANES), 1)
    fq_all = jnp.sum(jnp.where(lane == hd, fc_ref[...], 0.0), axis=-1, keepdims=True)
    fr = fr_ref[pl.ds(hd, 1), :]
    blk = lambda i: slice(i * tq, (i + 1) * tq)

    def scores(i, j):
        t = _nt_dot(q_ref[blk(i), :], k_ref[blk(j), :]) - fr[:, blk(j)]
        return _causal(t) if i == j else t

    pairs = [(i, j) for i in range(nq) for j in range(i + 1)]
    t_next = scores(*pairs[0])
    carry = None
    for n, (i, j) in enumerate(pairs):
        t = t_next
        if n + 1 < len(pairs):
            t_next = scores(*pairs[n + 1])
        if j == 0:
            carry = (jnp.full((tq, 1), NEG, F32), jnp.zeros((tq, 1), F32), jnp.zeros((tq, HEAD_DIM), F32))
        carry = _softmax_step(carry, t, fq_all[blk(i), :], v_ref[blk(j), :])
        if j == i:
            _, l, acc = carry
            o_ref[blk(i), :] = (acc / l * ga_ref[blk(i), :].astype(F32)).astype(BF16)


def _attn_p_call(q, k, v, f_col, f_row, ga, batch, seq):
    tq = 512
    head = pl.BlockSpec((seq, HEAD_DIM), lambda b, h: (b, h))
    return pl.pallas_call(
        functools.partial(_attn_p_kernel, seq=seq, tq=tq),
        grid=(batch, N_HEADS),
        in_specs=[
            head, head, head,
            pl.BlockSpec((seq, LANES), lambda b, h: (b, 0)),
            pl.BlockSpec((None, N_HEADS, seq), lambda b, h: (b, 0, 0)),
            head,
        ],
        out_specs=head,
        out_shape=jax.ShapeDtypeStruct((batch * seq, D_ATTN), BF16),
        compiler_params=_params(("parallel", "parallel"), 32),
        name="attn_prompt",
    )(q, k, v, f_col, f_row, ga)


def _attn_s_kernel(q_ref, ck_ref, cv_ref, g_ref, fnc_ref, fnr_ref, kn_ref, vn_ref, ga_ref,
                   o_ref, m_sc, l_sc, acc_sc, *, new, tk):
    j = pl.program_id(1)
    last = pl.num_programs(1) - 1

    @pl.when(j == 0)
    def _():
        m_sc[...] = jnp.full(m_sc.shape, NEG, F32)
        l_sc[...] = jnp.zeros(l_sc.shape, F32)
        acc_sc[...] = jnp.zeros(acc_sc.shape, F32)

    heads = range(N_HEADS)
    sls = [slice(hd * HEAD_DIM, (hd + 1) * HEAD_DIM) for hd in heads]
    state = [(m_sc[hd][:, 0:1], l_sc[hd][:, 0:1], acc_sc[:, sls[hd]]) for hd in heads]
    fq = [fnc_ref[:, hd:hd + 1] for hd in heads]

    ts = [_nt_dot(q_ref[:, sls[hd]], ck_ref[pl.ds(hd, tk, stride=N_HEADS), :].astype(BF16)) + g_ref[hd:hd + 1, :]
          for hd in heads]
    parts = [_softmax_parts(state[hd], ts[hd], fq[hd]) for hd in heads]
    out = []
    for hd in heads:
        m, l, acc, p = parts[hd]
        out.append((m, l, acc + _dot(p, cv_ref[pl.ds(hd, tk, stride=N_HEADS), :].astype(BF16))))
    for hd in heads:
        m, l, acc = out[hd]
        m_sc[hd] = jnp.broadcast_to(m, (new, LANES))
        l_sc[hd] = jnp.broadcast_to(l, (new, LANES))
        acc_sc[:, sls[hd]] = acc

    @pl.when(j == last)
    def _():
        pad = jnp.zeros((LANES - new, HEAD_DIM), BF16)
        ts = [_causal(_nt_dot(q_ref[:, sls[hd]], jnp.concatenate([kn_ref[:, sls[hd]], pad], axis=0))
                      - fnr_ref[hd:hd + 1, :]) for hd in heads]
        parts = [_softmax_parts(out[hd], ts[hd], fq[hd]) for hd in heads]
        for hd in heads:
            _, l, acc, p = parts[hd]
            acc = acc + _dot(p, jnp.concatenate([vn_ref[:, sls[hd]], pad], axis=0))
            o_ref[:, sls[hd]] = (acc / l * ga_ref[:, sls[hd]].astype(F32)).astype(BF16)


def _attn_s_call(q, cache_k, cache_v, g_row, fn_col, fn_row, k_new, v_new, ga, layer, batch, past, new):
    tk = 2048
    row = lambda width: pl.BlockSpec((new, width), lambda b, j: (b, 0))
    cache = pl.BlockSpec((None, None, tk * N_HEADS, HEAD_DIM), lambda b, j: (layer, b, j, 0))
    return pl.pallas_call(
        functools.partial(_attn_s_kernel, new=new, tk=tk),
        grid=(batch, past // tk),
        in_specs=[
            row(D_ATTN), cache, cache,
            pl.BlockSpec((None, N_HEADS, tk), lambda b, j: (b, 0, j)),
            row(LANES),
            pl.BlockSpec((None, N_HEADS, LANES), lambda b, j: (b, 0, 0)),
            row(D_ATTN), row(D_ATTN), row(D_ATTN),
        ],
        out_specs=row(D_ATTN),
        out_shape=jax.ShapeDtypeStruct((batch * new, D_ATTN), BF16),
        scratch_shapes=[
            pltpu.VMEM((N_HEADS, new, LANES), F32),
            pltpu.VMEM((N_HEADS, new, LANES), F32),
            pltpu.VMEM((new, D_ATTN), F32),
        ],
        compiler_params=_params(("parallel", "arbitrary"), 52),
        name="attn_sample",
    )(q, cache_k, cache_v, g_row, fn_col, fn_row, k_new, v_new, ga)


CONV_ROWS = 64


def _conv_kernel(u_ref, halo_ref, gb_ref, w_ref, cb_ref, lg_ref, lb_ref, o_ref, ext_sc, cv_sc,
                 *, tt, tiles_per_batch, halo_from_u):
    nc = D_CONV // LANES
    chunk = lambda c: slice(c * LANES, (c + 1) * LANES)
    for c in range(nc):
        ext_sc[c, HALO_PAD:HALO_PAD + tt, :] = u_ref[:, chunk(c)]
    if halo_from_u:
        first = lax.rem(pl.program_id(0), tiles_per_batch) == 0

        @pl.when(first)
        def _():
            ext_sc[:, 0:HALO_PAD, :] = jnp.zeros((nc, HALO_PAD, LANES), F32)

        @pl.when(jnp.logical_not(first))
        def _():
            for c in range(nc):
                ext_sc[c, 0:HALO_PAD, :] = halo_ref[:, chunk(c)]
    else:
        for c in range(nc):
            ext_sc[c, HALO_PAD - HALO:HALO_PAD, :] = halo_ref[:, chunk(c)]

    base = HALO_PAD - HALO
    for c in range(nc):
        w_taps = w_ref[:, chunk(c)]
        bias = cb_ref[:, chunk(c)]

        def rows_body(r, carry, c=c, w_taps=w_taps, bias=bias):
            r0 = pl.multiple_of(r * CONV_ROWS, CONV_ROWS)
            acc = jnp.zeros((CONV_ROWS, LANES), F32)
            for j in range(CONV_WIDTH):
                acc = acc + w_taps[j:j + 1, :] * ext_sc[c, pl.ds(r0 + (base + j), CONV_ROWS, stride=1), :]
            cv_sc[pl.ds(r0, CONV_ROWS), chunk(c)] = acc + bias
            return carry

        lax.fori_loop(0, tt // CONV_ROWS, rows_body, 0)

    cv = cv_sc[...]
    mu = jnp.mean(cv, axis=-1, keepdims=True)
    d = cv - mu
    var = jnp.mean(d * d, axis=-1, keepdims=True)
    y = d * lax.rsqrt(var + EPS) * lg_ref[...] + lb_ref[...]
    o_ref[...] = (_silu(y) * gb_ref[...].astype(F32)).astype(BF16)


def _conv_call(u, halo, gb, conv_w, conv_b, ln_g, ln_b, layer, tt, tiles_per_batch, halo_from_u):
    rows = u.shape[0]
    tile = pl.BlockSpec((tt, D_CONV), lambda t: (t, 0))
    if halo_from_u:
        per = tt // HALO_PAD
        halo_spec = pl.BlockSpec((HALO_PAD, D_CONV), lambda t: (jnp.maximum(t * per - 1, 0), 0))
    else:
        halo_spec = pl.BlockSpec((None, None, HALO, D_CONV), lambda t: (layer, t, 0, 0))
    kern = functools.partial(_conv_kernel, tt=tt, tiles_per_batch=tiles_per_batch, halo_from_u=halo_from_u)
    return pl.pallas_call(
        kern,
        grid=(rows // tt,),
        in_specs=[
            tile, halo_spec, tile,
            _layer_spec(conv_w.shape, layer), _layer_spec(conv_b.shape, layer),
            _layer_spec(ln_g.shape, layer), _layer_spec(ln_b.shape, layer),
        ],
        out_specs=tile,
        out_shape=jax.ShapeDtypeStruct((rows, D_CONV), BF16),
        scratch_shapes=[
            pltpu.VMEM((D_CONV // LANES, HALO_PAD + tt, LANES), F32),
            pltpu.VMEM((tt, D_CONV), F32),
        ],
        compiler_params=_params(("parallel",), 32),
        name="conv_module",
    )(u, halo, gb, conv_w, conv_b, ln_g, ln_b)


OUT_COLS = 512


def _out_kernel(ya_ref, yb_ref, mab_ref, x_ref, mod_ref, woa_ref, wob_ref, wout_ref, o_ref, mg_sc,
                *, tm, rows_per_batch, batch0):
    m = pl.program_id(0)
    ya = ya_ref[...]
    yb = yb_ref[...]
    for c in range(D_MODEL // OUT_COLS):
        cs = slice(c * OUT_COLS, (c + 1) * OUT_COLS)
        cs_b = slice(D_MODEL + c * OUT_COLS, D_MODEL + (c + 1) * OUT_COLS)
        pa = _dot(ya, woa_ref[:, cs])
        pb = _dot(yb, wob_ref[:, cs])
        merged = mab_ref[:, cs].astype(F32) * pa + mab_ref[:, cs_b].astype(F32) * pb
        mg_sc[:, cs] = merged.astype(BF16)
    out = _dot(mg_sc[...], wout_ref[...])
    sub = min(tm, rows_per_batch)
    for i in range(tm // sub):
        b = _batch_of(m * tm + i * sub, rows_per_batch, batch0)
        gate = mod_ref[pl.ds(b, 1), 2 * D_MODEL:3 * D_MODEL]
        rs = slice(i * sub, (i + 1) * sub)
        o_ref[rs, :] = x_ref[rs, :] + gate * out[rs, :]


def _out_call(ya, yb, mab, x, mod, w_oa, w_ob, w_out, layer, rows_per_batch, batch0):
    rows = x.shape[0]
    tm = 512
    row_blk = lambda width: pl.BlockSpec((tm, width), lambda m: (m, 0))
    kern = functools.partial(_out_kernel, tm=tm, rows_per_batch=rows_per_batch, batch0=batch0)
    return pl.pallas_call(
        kern,
        grid=(rows // tm,),
        in_specs=[
            row_blk(D_ATTN), row_blk(D_CONV), row_blk(2 * D_MODEL), row_blk(D_MODEL),
            _layer_spec(mod.shape, layer), _layer_spec(w_oa.shape, layer),
            _layer_spec(w_ob.shape, layer), _layer_spec(w_out.shape, layer),
        ],
        out_specs=row_blk(D_MODEL),
        out_shape=jax.ShapeDtypeStruct((rows, D_MODEL), F32),
        scratch_shapes=[pltpu.VMEM((tm, D_MODEL), BF16)],
        compiler_params=_params(("parallel",), 56),
        name="out_proj",
    )(ya, yb, mab, x, mod, w_oa, w_ob, w_out)


PREP_TILE = 1024
FORGET_COL = 3 * D_ATTN


def _prep_in_kernel(a_ref, b_ref, o_ref):
    i = pl.program_id(1)

    @pl.when(i < FORGET_COL // PREP_TILE)
    def _():
        o_ref[...] = a_ref[...].astype(BF16)

    @pl.when(i >= FORGET_COL // PREP_TILE)
    def _():
        o_ref[...] = jnp.concatenate([a_ref[:, N_HEADS:], b_ref[:, :N_HEADS]], axis=1).astype(BF16)


def _prep_in_call(w_in):
    depth, d, d_in = w_in.shape
    n_tiles = (d_in - N_HEADS) // PREP_TILE
    rows = d // 2
    per = PREP_TILE // LANES
    return pl.pallas_call(
        _prep_in_kernel,
        grid=(depth, n_tiles, d // rows),
        in_specs=[
            pl.BlockSpec((None, rows, PREP_TILE), lambda l, i, r: (l, r, i)),
            pl.BlockSpec((None, rows, LANES), lambda l, i, r: (l, r, (i + 1) * per)),
        ],
        out_specs=pl.BlockSpec((None, rows, PREP_TILE), lambda l, i, r: (l, r, i)),
        out_shape=jax.ShapeDtypeStruct((depth, d, n_tiles * PREP_TILE), BF16),
        compiler_params=_params(("parallel", "parallel", "parallel"), 32),
        name="prep_w_in",
    )(w_in, w_in)


def _cast_kernel(x_ref, o_ref):
    o_ref[...] = x_ref[...].astype(BF16)


def _cast_call(w):
    depth, k, n = w.shape
    rows = 512
    blk = pl.BlockSpec((None, rows, n), lambda l, r: (l, r, 0))
    return pl.pallas_call(
        _cast_kernel,
        grid=(depth, k // rows),
        in_specs=[blk],
        out_specs=blk,
        out_shape=jax.ShapeDtypeStruct(w.shape, BF16),
        compiler_params=_params(("parallel", "parallel"), 32),
        name="cast_bf16",
    )(w)


def _layer_group(x, mod, rows_per_batch, batch0, wl, layer, depth, stacked_prev):
    h = _norm_call(x, mod, wl["norm_g"], layer, rows_per_batch, batch0)
    lf, lf_all, q, k_all, v_all, kb, vb, ga = _in_a_call(
        h, wl["w_all"], wl["w_f"], wl["b_f"], wl["q_g"], wl["k_g"], layer, depth, stacked_prev)
    u, gb, mab = _in_b_call(h, wl["w_all"], layer)
    return lf, q, (lf_all, k_all, v_all), kb, vb, ga, u, gb, mab


def kernel(x_prompt, x_sample, c_prompt, c_sample, cache_k, cache_v, cache_logf, state_conv, w_ada, b_ada, norm_g, w_in, b_f, q_norm_g, k_norm_g, conv_w, conv_b, conv_ln_g, conv_ln_b, w_oa, w_ob, w_out):
    batch, seq, d = x_prompt.shape
    dec_batch, dec_seq, _ = x_sample.shape
    depth = w_ada.shape[0]
    past = cache_k.shape[2]
    assert d == D_MODEL and seq % 1024 == 0 and (dec_batch * dec_seq) % 1024 == 0
    assert dec_seq >= HALO and seq >= HALO and dec_seq <= LANES

    nb = batch + dec_batch
    nb_pad = -(-nb // 8) * 8
    c_all = jnp.concatenate([c_prompt, c_sample, jnp.zeros((nb_pad - nb, d), F32)], axis=0)
    mod_all = _ada_call(c_all, w_ada, b_ada)

    xp = x_prompt.reshape(batch * seq, d)
    xs = x_sample.reshape(dec_batch * dec_seq, d)
    cache_k2 = cache_k.reshape(depth, dec_batch, past * N_HEADS, HEAD_DIM)
    cache_v2 = cache_v.reshape(depth, dec_batch, past * N_HEADS, HEAD_DIM)
    cache_lf_t = jnp.swapaxes(cache_logf, 2, 3)

    w_all = _prep_in_call(w_in)
    w_f = jnp.pad(w_in[:, :, FORGET_COL:FORGET_COL + N_HEADS],
                  ((0, 0), (0, 0), (0, LANES - N_HEADS))).astype(BF16)
    woa, wob, wout = _cast_call(w_oa), _cast_call(w_ob), _cast_call(w_out)
    wl = dict(
        norm_g=norm_g.reshape(depth, 1, d), w_all=w_all, w_f=w_f,
        b_f=jnp.pad(b_f, ((0, 0), (0, LANES - N_HEADS))).reshape(depth, 1, LANES),
        q_g=q_norm_g.reshape(depth, 1, HEAD_DIM), k_g=k_norm_g.reshape(depth, 1, HEAD_DIM),
    )
    cb = conv_b.reshape(depth, 1, D_CONV)
    lg, lb = conv_ln_g.reshape(depth, 1, D_CONV), conv_ln_b.reshape(depth, 1, D_CONV)

    conv_p, conv_s = [], []
    st_p = st_s = None
    for l in range(depth):
        lf, q, st_p, kb, vb, ga, u, gb, mab = _layer_group(xp, mod_all, seq, 0, wl, l, depth, st_p)
        f_col, f_row = _cum_p_call(lf, batch, seq)
        ya = _attn_p_call(q, kb, vb, f_col, f_row, ga, batch, seq)
        yb = _conv_call(u, u, gb, conv_w, cb, lg, lb, l, 256, seq // 256, True)
        xp = _out_call(ya, yb, mab, xp, mod_all, woa, wob, wout, l, seq, 0)
        conv_p.append(u.reshape(batch, seq, D_CONV)[:, seq - HALO:, :])

        lf, q, st_s, kb, vb, ga, u, gb, mab = _layer_group(xs, mod_all, dec_seq, batch, wl, l, depth, st_s)
        g_row, fn_col, fn_row = _cum_s_call(cache_lf_t, lf, l, dec_batch, past, dec_seq)
        ya = _attn_s_call(q, cache_k2, cache_v2, g_row, fn_col, fn_row, kb, vb, ga,
                          l, dec_batch, past, dec_seq)
        yb = _conv_call(u, state_conv, gb, conv_w, cb, lg, lb, l, dec_seq, 1, False)
        xs = _out_call(ya, yb, mab, xs, mod_all, woa, wob, wout, l, dec_seq, batch)
        conv_s.append(u.reshape(dec_batch, dec_seq, D_CONV)[:, dec_seq - HALO:, :])

    cache_p = (depth, batch, seq, N_HEADS, HEAD_DIM)
    cache_s = (depth, dec_batch, dec_seq, N_HEADS, HEAD_DIM)
    return (xp.reshape(batch, seq, d), xs.reshape(dec_batch, dec_seq, d),
            st_p[1].reshape(cache_p), st_p[2].reshape(cache_p),
            st_p[0].reshape(depth, batch, seq, N_HEADS), jnp.stack(conv_p),
            st_s[1].reshape(cache_s), st_s[2].reshape(cache_s),
            st_s[0].reshape(depth, dec_batch, dec_seq, N_HEADS), jnp.stack(conv_s))
```

```python
import functools

import jax
import jax.numpy as jnp
from jax import lax
from jax.experimental import pallas as pl
from jax.experimental.pallas import tpu as pltpu

D_MODEL = 2048
N_HEADS = 8
HEAD_DIM = 128
D_ATTN = N_HEADS * HEAD_DIM
D_CONV = 1024
CONV_WIDTH = 31
HALO = CONV_WIDTH - 1
EPS = 1e-6
NEG = -1e30
LOG2E = 1.4426950408889634

LANES = 128
HALO_PAD = 32
CUM_CHUNK = 256
MIB = 1024 * 1024

BF16 = jnp.bfloat16
F32 = jnp.float32


def _params(sem, vmem_mib):
    return pltpu.CompilerParams(dimension_semantics=sem, vmem_limit_bytes=vmem_mib * MIB)


def _layer_spec(shape, layer):
    nd = len(shape) - 1
    return pl.BlockSpec((None,) + tuple(shape[1:]), lambda *_: (layer,) + (0,) * nd, pipeline_mode=pl.Buffered(1))


def _sigmoid(x):
    return 1.0 / (1.0 + jnp.exp(-x))


def _silu(x):
    return x * _sigmoid(x)


def _log_sigmoid(x):
    return jnp.minimum(x, 0.0) - jnp.log1p(jnp.exp(-jnp.abs(x)))


def _nt_dot(a, b):
    return lax.dot_general(a, b, (((1,), (1,)), ((), ())), preferred_element_type=F32)


def _dot(a, b):
    return jnp.dot(a, b, preferred_element_type=F32)


def _split3(x):
    hi = x.astype(BF16)
    r1 = x - hi.astype(F32)
    mid = r1.astype(BF16)
    lo = (r1 - mid.astype(F32)).astype(BF16)
    return hi, mid, lo


def _ada_kernel(c_ref, w_ref, b_ref, o_ref):
    s = _silu(c_ref[...]).astype(BF16)
    o_ref[...] = _dot(s, w_ref[...].astype(BF16)) + b_ref[...]


def _ada_call(c_all, w_ada, b_ada):
    depth, d, e = w_ada.shape
    nb = c_all.shape[0]
    tn = 1024
    return pl.pallas_call(
        _ada_kernel,
        grid=(depth, e // tn),
        in_specs=[
            pl.BlockSpec((nb, d), lambda l, n: (0, 0)),
            pl.BlockSpec((None, d, tn), lambda l, n: (l, 0, n)),
            pl.BlockSpec((None, 1, tn), lambda l, n: (l, 0, n)),
        ],
        out_specs=pl.BlockSpec((None, nb, tn), lambda l, n: (l, 0, n)),
        out_shape=jax.ShapeDtypeStruct((depth, nb, e), F32),
        compiler_params=_params(("parallel", "parallel"), 32),
        name="ada_mod",
    )(c_all, w_ada, b_ada.reshape(depth, 1, e))


def _batch_of(row, rows_per_batch, batch0):
    return batch0 + lax.div(row, rows_per_batch)


def _norm_kernel(x_ref, mod_ref, g_ref, h_ref, *, tm, rows_per_batch, batch0):
    m = pl.program_id(0)
    sub = min(tm, rows_per_batch)
    for i in range(tm // sub):
        b = _batch_of(m * tm + i * sub, rows_per_batch, batch0)
        shift = mod_ref[pl.ds(b, 1), 0:D_MODEL]
        scale = mod_ref[pl.ds(b, 1), D_MODEL:2 * D_MODEL]
        x = x_ref[i * sub:(i + 1) * sub, :]
        ms = jnp.mean(x * x, axis=-1, keepdims=True)
        y = x * lax.rsqrt(ms + EPS) * g_ref[...]
        h_ref[i * sub:(i + 1) * sub, :] = (y * (1.0 + scale) + shift).astype(BF16)


def _norm_call(x, mod, g, layer, rows_per_batch, batch0):
    rows = x.shape[0]
    tm = 1024
    kern = functools.partial(_norm_kernel, tm=tm, rows_per_batch=rows_per_batch, batch0=batch0)
    return pl.pallas_call(
        kern,
        grid=(rows // tm,),
        in_specs=[
            pl.BlockSpec((tm, D_MODEL), lambda m: (m, 0)),
            _layer_spec(mod.shape, layer),
            _layer_spec(g.shape, layer),
        ],
        out_specs=pl.BlockSpec((tm, D_MODEL), lambda m: (m, 0)),
        out_shape=jax.ShapeDtypeStruct((rows, D_MODEL), BF16),
        compiler_params=_params(("parallel",), 32),
        name="norm_mod",
    )(x, mod, g)


def _head_rmsnorm(a, g):
    ms = jnp.mean(a * a, axis=-1, keepdims=True)
    return a * lax.rsqrt(ms + EPS) * g


def _store_heads(y, hd, tm, cache_ref, bf_ref):
    cache_ref[pl.ds(hd, tm, stride=N_HEADS), :] = y
    bf_ref[:, hd * HEAD_DIM:(hd + 1) * HEAD_DIM] = y.astype(BF16)


def _in_a_kernel(*refs, tm, n_alias):
    (h_ref, w_ref, wf_ref, bf_ref, qg_ref, kg_ref) = refs[:6]
    (lf_ref, lf8_ref, q_ref, k_ref, v_ref, kb_ref, vb_ref, ga_ref) = refs[6 + n_alias:]
    n = pl.program_id(1)

    def head(acc, hd):
        return acc[:, hd * HEAD_DIM:(hd + 1) * HEAD_DIM]

    @pl.when(n == 0)
    def _():
        f = _dot(h_ref[...], wf_ref[...]) + bf_ref[...]
        lf = _log_sigmoid(f)
        lf_ref[...] = lf
        lf8_ref[...] = lf[:, 0:N_HEADS]
        acc = _dot(h_ref[...], w_ref[...])
        for hd in range(N_HEADS):
            y = _head_rmsnorm(head(acc, hd), qg_ref[...]) * (HEAD_DIM ** -0.5 * LOG2E)
            q_ref[:, hd * HEAD_DIM:(hd + 1) * HEAD_DIM] = y.astype(BF16)

    @pl.when(n == 1)
    def _():
        acc = _dot(h_ref[...], w_ref[...])
        for hd in range(N_HEADS):
            _store_heads(_head_rmsnorm(head(acc, hd), kg_ref[...]), hd, tm, k_ref, kb_ref)

    @pl.when(n == 2)
    def _():
        acc = _dot(h_ref[...], w_ref[...])
        for hd in range(N_HEADS):
            _store_heads(head(acc, hd), hd, tm, v_ref, vb_ref)

    @pl.when(n == 3)
    def _():
        ga_ref[...] = _silu(_dot(h_ref[...], w_ref[...])).astype(BF16)


def _in_a_call(h, w_all, w_f, b_f, q_g, k_g, layer, depth, stacked_prev):
    rows = h.shape[0]
    tm, tn = 1024, 1024
    row_blk = lambda width: pl.BlockSpec((tm, width), lambda m, n: (m, 0))
    cache_blk = pl.BlockSpec((None, tm * N_HEADS, HEAD_DIM), lambda m, n: (layer, m, 0))
    cache_shape = jax.ShapeDtypeStruct((depth, rows * N_HEADS, HEAD_DIM), F32)
    alias_in = [] if stacked_prev is None else list(stacked_prev)
    n_in = 6
    return pl.pallas_call(
        functools.partial(_in_a_kernel, tm=tm, n_alias=len(alias_in)),
        grid=(rows // tm, 4),
        in_specs=[
            row_blk(D_MODEL),
            pl.BlockSpec((None, D_MODEL, tn), lambda m, n: (layer, 0, n)),
            _layer_spec(w_f.shape, layer),
            _layer_spec(b_f.shape, layer),
            _layer_spec(q_g.shape, layer),
            _layer_spec(k_g.shape, layer),
        ] + [pl.BlockSpec(memory_space=pl.ANY)] * len(alias_in),
        out_specs=[row_blk(LANES),
                   pl.BlockSpec((None, tm, N_HEADS), lambda m, n: (layer, m, 0)),
                   row_blk(D_ATTN), cache_blk, cache_blk,
                   row_blk(D_ATTN), row_blk(D_ATTN), row_blk(D_ATTN)],
        out_shape=[
            jax.ShapeDtypeStruct((rows, LANES), F32),
            jax.ShapeDtypeStruct((depth, rows, N_HEADS), F32),
            jax.ShapeDtypeStruct((rows, D_ATTN), BF16),
            cache_shape,
            cache_shape,
            jax.ShapeDtypeStruct((rows, D_ATTN), BF16),
            jax.ShapeDtypeStruct((rows, D_ATTN), BF16),
            jax.ShapeDtypeStruct((rows, D_ATTN), BF16),
        ],
        input_output_aliases={n_in + i: out for i, out in zip(range(len(alias_in)), (1, 3, 4))},
        compiler_params=_params(("parallel", "arbitrary"), 58),
        name="in_proj_a",
    )(h, w_all, w_f, b_f, q_g, k_g, *alias_in)


IN_A_TILES = 4


def _in_b_kernel(h_ref, w_ref, u_ref, gb_ref, mab_ref, lin_sc):
    n = pl.program_id(1)

    @pl.when(n == 0)
    def _():
        lin_sc[...] = _dot(h_ref[...], w_ref[...])

    @pl.when(n == 1)
    def _():
        u_ref[...] = lin_sc[...] * _sigmoid(_dot(h_ref[...], w_ref[...]))

    @pl.when(n == 2)
    def _():
        gb_ref[...] = _silu(_dot(h_ref[...], w_ref[...])).astype(BF16)

    @pl.when(n > 2)
    def _():
        mab_ref[...] = _sigmoid(_dot(h_ref[...], w_ref[...])).astype(BF16)


def _in_b_call(h, w_all, layer):
    rows = h.shape[0]
    tm, tn = 1024, 1024
    return pl.pallas_call(
        _in_b_kernel,
        grid=(rows // tm, 7),
        in_specs=[
            pl.BlockSpec((tm, D_MODEL), lambda m, n: (m, 0)),
            pl.BlockSpec((None, D_MODEL, tn), lambda m, n: (layer, 0, n + IN_A_TILES)),
        ],
        out_specs=[
            pl.BlockSpec((tm, D_CONV), lambda m, n: (m, 0)),
            pl.BlockSpec((tm, D_CONV), lambda m, n: (m, 0)),
            pl.BlockSpec((tm, tn), lambda m, n: (m, jnp.clip(n - 3, 0, 3))),
        ],
        out_shape=[
            jax.ShapeDtypeStruct((rows, D_CONV), F32),
            jax.ShapeDtypeStruct((rows, D_CONV), BF16),
            jax.ShapeDtypeStruct((rows, 2 * D_MODEL), BF16),
        ],
        scratch_shapes=[pltpu.VMEM((tm, D_CONV), F32)],
        compiler_params=_params(("parallel", "arbitrary"), 48),
        name="in_proj_b",
    )(h, w_all)


def _tri(n, upper):
    r = lax.broadcasted_iota(jnp.int32, (n, n), 0)
    c = lax.broadcasted_iota(jnp.int32, (n, n), 1)
    keep = (r <= c) if upper else (c <= r)
    return jnp.where(keep, 1.0, 0.0).astype(BF16)


def _cum_p_kernel(lf_ref, fc_ref, fr_ref, *, seq):
    tril = _tri(CUM_CHUNK, upper=False)
    carry = jnp.zeros((1, LANES), F32)
    for i in range(seq // CUM_CHUNK):
        sl = slice(i * CUM_CHUNK, (i + 1) * CUM_CHUNK)
        hi, mid, lo = _split3(lf_ref[sl, :])
        y = _dot(tril, hi) + _dot(tril, mid) + _dot(tril, lo) + carry
        carry = y[CUM_CHUNK - 1:CUM_CHUNK, :]
        y2 = y * LOG2E
        fc_ref[sl, :] = y2
        fr_ref[:, sl] = y2.T[0:N_HEADS, :]


def _cum_p_call(lf, batch, seq):
    kern = functools.partial(_cum_p_kernel, seq=seq)
    return pl.pallas_call(
        kern,
        grid=(batch,),
        in_specs=[pl.BlockSpec((seq, LANES), lambda b: (b, 0))],
        out_specs=[
            pl.BlockSpec((seq, LANES), lambda b: (b, 0)),
            pl.BlockSpec((None, N_HEADS, seq), lambda b: (b, 0, 0)),
        ],
        out_shape=[
            jax.ShapeDtypeStruct((batch * seq, LANES), F32),
            jax.ShapeDtypeStruct((batch, N_HEADS, seq), F32),
        ],
        compiler_params=_params(("parallel",), 32),
        name="cum_prompt",
    )(lf)


def _cum_s_kernel(clf_ref, lfn_ref, g_ref, fnc_ref, fnr_ref, f_sc, *, past, new):
    triu = _tri(CUM_CHUNK, upper=True)
    carry = jnp.zeros((N_HEADS, 1), F32)
    for i in range(past // CUM_CHUNK):
        sl = slice(i * CUM_CHUNK, (i + 1) * CUM_CHUNK)
        hi, mid, lo = _split3(clf_ref[:, sl])
        y = _dot(hi, triu) + _dot(mid, triu) + _dot(lo, triu) + carry
        carry = y[:, CUM_CHUNK - 1:CUM_CHUNK]
        f_sc[:, sl] = y
    g_ref[...] = (carry - f_sc[...]) * LOG2E
    tril = _tri(LANES, upper=False)
    xn = jnp.concatenate([lfn_ref[...], jnp.zeros((LANES - new, LANES), F32)], axis=0)
    hi, mid, lo = _split3(xn)
    yn = (_dot(tril, hi) + _dot(tril, mid) + _dot(tril, lo)) * LOG2E
    fnc_ref[...] = yn[0:new, :]
    fnr_ref[...] = yn.T[0:N_HEADS, :]


def _cum_s_call(clf_t, lf_new, layer, batch, past, new):
    kern = functools.partial(_cum_s_kernel, past=past, new=new)
    return pl.pallas_call(
        kern,
        grid=(batch,),
        in_specs=[
            pl.BlockSpec((None, None, N_HEADS, past), lambda b: (layer, b, 0, 0)),
            pl.BlockSpec((new, LANES), lambda b: (b, 0)),
        ],
        out_specs=[
            pl.BlockSpec((None, N_HEADS, past), lambda b: (b, 0, 0)),
            pl.BlockSpec((new, LANES), lambda b: (b, 0)),
            pl.BlockSpec((None, N_HEADS, LANES), lambda b: (b, 0, 0)),
        ],
        out_shape=[
            jax.ShapeDtypeStruct((batch, N_HEADS, past), F32),
            jax.ShapeDtypeStruct((batch * new, LANES), F32),
            jax.ShapeDtypeStruct((batch, N_HEADS, LANES), F32),
        ],
        scratch_shapes=[pltpu.VMEM((N_HEADS, past), F32)],
        compiler_params=_params(("parallel",), 32),
        name="cum_sample",
    )(clf_t, lf_new)


def _softmax_parts(carry, t, fq):
    m, l, acc = carry
    m_new = jnp.maximum(m, jnp.max(t, axis=-1, keepdims=True) + fq)
    alpha = jnp.exp2(m - m_new)
    p = jnp.exp2(t + (fq - m_new))
    return m_new, alpha * l + jnp.sum(p, axis=-1, keepdims=True), alpha * acc, p.astype(BF16)


def _softmax_step(carry, t, fq, v):
    m_new, l, acc, p = _softmax_parts(carry, t, fq)
    return m_new, l, acc + _dot(p, v)


def _causal(s):
    row = lax.broadcasted_iota(jnp.int32, s.shape, 0)
    col = lax.broadcasted_iota(jnp.int32, s.shape, 1)
    return jnp.where(col <= row, s, NEG)


def _attn_p_kernel(q_ref, k_ref, v_ref, fc_ref, fr_ref, ga_ref, o_ref, *, seq, tq):
    hd = pl.program_id(1)
    nq = seq // tq
    lane = lax.broadcasted_iota(jnp.int32, (seq, LANES), 1)
    fq_all = jnp.sum(jnp.where(lane == hd, fc_ref[...], 0.0), axis=-1, keepdims=True)
    fr = fr_ref[pl.ds(hd, 1), :]
    blk = lambda i: slice(i * tq, (i + 1) * tq)

    def scores(i, j):
        t = _nt_dot(q_ref[blk(i), :], k_ref[blk(j), :]) - fr[:, blk(j)]
        return _causal(t) if i == j else t

    pairs = [(i, j) for i in range(nq) for j in range(i + 1)]
    t_next = scores(*pairs[0])
    carry = None
    for n, (i, j) in enumerate(pairs):
        t = t_next
        if n + 1 < len(pairs):
            t_next = scores(*pairs[n + 1])
        if j == 0:
            carry = (jnp.full((tq, 1), NEG, F32), jnp.zeros((tq, 1), F32), jnp.zeros((tq, HEAD_DIM), F32))
        carry = _softmax_step(carry, t, fq_all[blk(i), :], v_ref[blk(j), :])
        if j == i:
            _, l, acc = carry
            o_ref[blk(i), :] = (acc / l * ga_ref[blk(i), :].astype(F32)).astype(BF16)


def _attn_p_call(q, k, v, f_col, f_row, ga, batch, seq):
    tq = 512
    head = pl.BlockSpec((seq, HEAD_DIM), lambda b, h: (b, h))
    return pl.pallas_call(
        functools.partial(_attn_p_kernel, seq=seq, tq=tq),
        grid=(batch, N_HEADS),
        in_specs=[
            head, head, head,
            pl.BlockSpec((seq, LANES), lambda b, h: (b, 0)),
            pl.BlockSpec((None, N_HEADS, seq), lambda b, h: (b, 0, 0)),
            head,
        ],
        out_specs=head,
        out_shape=jax.ShapeDtypeStruct((batch * seq, D_ATTN), BF16),
        compiler_params=_params(("parallel", "parallel"), 32),
        name="attn_prompt",
    )(q, k, v, f_col, f_row, ga)


def _attn_s_kernel(q_ref, ck_ref, cv_ref, g_ref, fnc_ref, fnr_ref, kn_ref, vn_ref, ga_ref,
                   o_ref, m_sc, l_sc, acc_sc, *, new, tk):
    j = pl.program_id(1)
    last = pl.num_programs(1) - 1

    @pl.when(j == 0)
    def _():
        m_sc[...] = jnp.full(m_sc.shape, NEG, F32)
        l_sc[...] = jnp.zeros(l_sc.shape, F32)
        acc_sc[...] = jnp.zeros(acc_sc.shape, F32)

    heads = range(N_HEADS)
    sls = [slice(hd * HEAD_DIM, (hd + 1) * HEAD_DIM) for hd in heads]
    state = [(m_sc[hd][:, 0:1], l_sc[hd][:, 0:1], acc_sc[:, sls[hd]]) for hd in heads]
    fq = [fnc_ref[:, hd:hd + 1] for hd in heads]

    ts = [_nt_dot(q_ref[:, sls[hd]], ck_ref[pl.ds(hd, tk, stride=N_HEADS), :].astype(BF16)) + g_ref[hd:hd + 1, :]
          for hd in heads]
    parts = [_softmax_parts(state[hd], ts[hd], fq[hd]) for hd in heads]
    out = []
    for hd in heads:
        m, l, acc, p = parts[hd]
        out.append((m, l, acc + _dot(p, cv_ref[pl.ds(hd, tk, stride=N_HEADS), :].astype(BF16))))
    for hd in heads:
        m, l, acc = out[hd]
        m_sc[hd] = jnp.broadcast_to(m, (new, LANES))
        l_sc[hd] = jnp.broadcast_to(l, (new, LANES))
        acc_sc[:, sls[hd]] = acc

    @pl.when(j == last)
    def _():
        pad = jnp.zeros((LANES - new, HEAD_DIM), BF16)
        ts = [_causal(_nt_dot(q_ref[:, sls[hd]], jnp.concatenate([kn_ref[:, sls[hd]], pad], axis=0))
                      - fnr_ref[hd:hd + 1, :]) for hd in heads]
        parts = [_softmax_parts(out[hd], ts[hd], fq[hd]) for hd in heads]
        for hd in heads:
            _, l, acc, p = parts[hd]
            acc = acc + _dot(p, jnp.concatenate([vn_ref[:, sls[hd]], pad], axis=0))
            o_ref[:, sls[hd]] = (acc / l * ga_ref[:, sls[hd]].astype(F32)).astype(BF16)


def _attn_s_call(q, cache_k, cache_v, g_row, fn_col, fn_row, k_new, v_new, ga, layer, batch, past, new):
    tk = 2048
    row = lambda width: pl.BlockSpec((new, width), lambda b, j: (b, 0))
    cache = pl.BlockSpec((None, None, tk * N_HEADS, HEAD_DIM), lambda b, j: (layer, b, j, 0))
    return pl.pallas_call(
        functools.partial(_attn_s_kernel, new=new, tk=tk),
        grid=(batch, past // tk),
        in_specs=[
            row(D_ATTN), cache, cache,
            pl.BlockSpec((None, N_HEADS, tk), lambda b, j: (b, 0, j)),
            row(LANES),
            pl.BlockSpec((None, N_HEADS, LANES), lambda b, j: (b, 0, 0)),
            row(D_ATTN), row(D_ATTN), row(D_ATTN),
        ],
        out_specs=row(D_ATTN),
        out_shape=jax.ShapeDtypeStruct((batch * new, D_ATTN), BF16),
        scratch_shapes=[
            pltpu.VMEM((N_HEADS, new, LANES), F32),
            pltpu.VMEM((N_HEADS, new, LANES), F32),
            pltpu.VMEM((new, D_ATTN), F32),
        ],
        compiler_params=_params(("parallel", "arbitrary"), 52),
        name="attn_sample",
    )(q, cache_k, cache_v, g_row, fn_col, fn_row, k_new, v_new, ga)


CONV_ROWS = 64


def _conv_kernel(u_ref, halo_ref, gb_ref, w_ref, cb_ref, lg_ref, lb_ref, o_ref, ext_sc, cv_sc,
                 *, tt, tiles_per_batch, halo_from_u):
    nc = D_CONV // LANES
    chunk = lambda c: slice(c * LANES, (c + 1) * LANES)
    for c in range(nc):
        ext_sc[c, HALO_PAD:HALO_PAD + tt, :] = u_ref[:, chunk(c)]
    if halo_from_u:
        first = lax.rem(pl.program_id(0), tiles_per_batch) == 0

        @pl.when(first)
        def _():
            ext_sc[:, 0:HALO_PAD, :] = jnp.zeros((nc, HALO_PAD, LANES), F32)

        @pl.when(jnp.logical_not(first))
        def _():
            for c in range(nc):
                ext_sc[c, 0:HALO_PAD, :] = halo_ref[:, chunk(c)]
    else:
        for c in range(nc):
            ext_sc[c, HALO_PAD - HALO:HALO_PAD, :] = halo_ref[:, chunk(c)]

    base = HALO_PAD - HALO
    for c in range(nc):
        w_taps = w_ref[:, chunk(c)]
        bias = cb_ref[:, chunk(c)]

        def rows_body(r, carry, c=c, w_taps=w_taps, bias=bias):
            r0 = pl.multiple_of(r * CONV_ROWS, CONV_ROWS)
            acc = jnp.zeros((CONV_ROWS, LANES), F32)
            for j in range(CONV_WIDTH):
                acc = acc + w_taps[j:j + 1, :] * ext_sc[c, pl.ds(r0 + (base + j), CONV_ROWS, stride=1), :]
            cv_sc[pl.ds(r0, CONV_ROWS), chunk(c)] = acc + bias
            return carry

        lax.fori_loop(0, tt // CONV_ROWS, rows_body, 0)

    cv = cv_sc[...]
    mu = jnp.mean(cv, axis=-1, keepdims=True)
    d = cv - mu
    var = jnp.mean(d * d, axis=-1, keepdims=True)
    y = d * lax.rsqrt(var + EPS) * lg_ref[...] + lb_ref[...]
    o_ref[...] = (_silu(y) * gb_ref[...].astype(F32)).astype(BF16)


def _conv_call(u, halo, gb, conv_w, conv_b, ln_g, ln_b, layer, tt, tiles_per_batch, halo_from_u):
    rows = u.shape[0]
    tile = pl.BlockSpec((tt, D_CONV), lambda t: (t, 0))
    if halo_from_u:
        per = tt // HALO_PAD
        halo_spec = pl.BlockSpec((HALO_PAD, D_CONV), lambda t: (jnp.maximum(t * per - 1, 0), 0))
    else:
        halo_spec = pl.BlockSpec((None, None, HALO, D_CONV), lambda t: (layer, t, 0, 0))
    kern = functools.partial(_conv_kernel, tt=tt, tiles_per_batch=tiles_per_batch, halo_from_u=halo_from_u)
    return pl.pallas_call(
        kern,
        grid=(rows // tt,),
        in_specs=[
            tile, halo_spec, tile,
            _layer_spec(conv_w.shape, layer), _layer_spec(conv_b.shape, layer),
            _layer_spec(ln_g.shape, layer), _layer_spec(ln_b.shape, layer),
        ],
        out_specs=tile,
        out_shape=jax.ShapeDtypeStruct((rows, D_CONV), BF16),
        scratch_shapes=[
            pltpu.VMEM((D_CONV // LANES, HALO_PAD + tt, LANES), F32),
            pltpu.VMEM((tt, D_CONV), F32),
        ],
        compiler_params=_params(("parallel",), 32),
        name="conv_module",
    )(u, halo, gb, conv_w, conv_b, ln_g, ln_b)


OUT_COLS = 512


def _out_kernel(ya_ref, yb_ref, mab_ref, x_ref, mod_ref, woa_ref, wob_ref, wout_ref, o_ref, mg_sc,
                *, tm, rows_per_batch, batch0):
    m = pl.program_id(0)
    ya = ya_ref[...]
    yb = yb_ref[...]
    for c in range(D_MODEL // OUT_COLS):
        cs = slice(c * OUT_COLS, (c + 1) * OUT_COLS)
        cs_b = slice(D_MODEL + c * OUT_COLS, D_MODEL + (c + 1) * OUT_COLS)
        pa = _dot(ya, woa_ref[:, cs])
        pb = _dot(yb, wob_ref[:, cs])
        merged = mab_ref[:, cs].astype(F32) * pa + mab_ref[:, cs_b].astype(F32) * pb
        mg_sc[:, cs] = merged.astype(BF16)
    out = _dot(mg_sc[...], wout_ref[...])
    sub = min(tm, rows_per_batch)
    for i in range(tm // sub):
        b = _batch_of(m * tm + i * sub, rows_per_batch, batch0)
        gate = mod_ref[pl.ds(b, 1), 2 * D_MODEL:3 * D_MODEL]
        rs = slice(i * sub, (i + 1) * sub)
        o_ref[rs, :] = x_ref[rs, :] + gate * out[rs, :]


def _out_call(ya, yb, mab, x, mod, w_oa, w_ob, w_out, layer, rows_per_batch, batch0):
    rows = x.shape[0]
    tm = 512
    row_blk = lambda width: pl.BlockSpec((tm, width), lambda m: (m, 0))
    kern = functools.partial(_out_kernel, tm=tm, rows_per_batch=rows_per_batch, batch0=batch0)
    return pl.pallas_call(
        kern,
        grid=(rows // tm,),
        in_specs=[
            row_blk(D_ATTN), row_blk(D_CONV), row_blk(2 * D_MODEL), row_blk(D_MODEL),
            _layer_spec(mod.shape, layer), _layer_spec(w_oa.shape, layer),
            _layer_spec(w_ob.shape, layer), _layer_spec(w_out.shape, layer),
        ],
        out_specs=row_blk(D_MODEL),
        out_shape=jax.ShapeDtypeStruct((rows, D_MODEL), F32),
        scratch_shapes=[pltpu.VMEM((tm, D_MODEL), BF16)],
        compiler_params=_params(("parallel",), 56),
        name="out_proj",
    )(ya, yb, mab, x, mod, w_oa, w_ob, w_out)


PREP_TILE = 1024
FORGET_COL = 3 * D_ATTN


def _prep_in_kernel(a_ref, b_ref, o_ref):
    i = pl.program_id(1)

    @pl.when(i < FORGET_COL // PREP_TILE)
    def _():
        o_ref[...] = a_ref[...].astype(BF16)

    @pl.when(i >= FORGET_COL // PREP_TILE)
    def _():
        o_ref[...] = jnp.concatenate([a_ref[:, N_HEADS:], b_ref[:, :N_HEADS]], axis=1).astype(BF16)


def _prep_in_call(w_in):
    depth, d, d_in = w_in.shape
    n_tiles = (d_in - N_HEADS) // PREP_TILE
    first = FORGET_COL // PREP_TILE
    starts = [(i + 1) * PREP_TILE for i in range(first, n_tiles)]
    tails = jnp.stack([w_in[:, :, s:s + N_HEADS] for s in starts], axis=1)
    tails = jnp.pad(tails, ((0, 0), (0, 0), (0, 0), (0, LANES - N_HEADS)))
    rows = d // 2
    return pl.pallas_call(
        _prep_in_kernel,
        grid=(depth, n_tiles, d // rows),
        in_specs=[
            pl.BlockSpec((None, rows, PREP_TILE), lambda l, i, r: (l, r, i)),
            pl.BlockSpec((None, None, rows, LANES), lambda l, i, r: (l, jnp.maximum(i - first, 0), r, 0)),
        ],
        out_specs=pl.BlockSpec((None, rows, PREP_TILE), lambda l, i, r: (l, r, i)),
        out_shape=jax.ShapeDtypeStruct((depth, d, n_tiles * PREP_TILE), BF16),
        compiler_params=_params(("parallel", "parallel", "parallel"), 32),
        name="prep_w_in",
    )(w_in, tails)


def _cast_kernel(x_ref, o_ref):
    o_ref[...] = x_ref[...].astype(BF16)


def _cast_call(w):
    depth, k, n = w.shape
    rows = 512
    blk = pl.BlockSpec((None, rows, n), lambda l, r: (l, r, 0))
    return pl.pallas_call(
        _cast_kernel,
        grid=(depth, k // rows),
        in_specs=[blk],
        out_specs=blk,
        out_shape=jax.ShapeDtypeStruct(w.shape, BF16),
        compiler_params=_params(("parallel", "parallel"), 32),
        name="cast_bf16",
    )(w)


def _layer_group(x, mod, rows_per_batch, batch0, wl, layer, depth, stacked_prev):
    h = _norm_call(x, mod, wl["norm_g"], layer, rows_per_batch, batch0)
    lf, lf_all, q, k_all, v_all, kb, vb, ga = _in_a_call(
        h, wl["w_all"], wl["w_f"], wl["b_f"], wl["q_g"], wl["k_g"], layer, depth, stacked_prev)
    u, gb, mab = _in_b_call(h, wl["w_all"], layer)
    return lf, q, (lf_all, k_all, v_all), kb, vb, ga, u, gb, mab


def kernel(x_prompt, x_sample, c_prompt, c_sample, cache_k, cache_v, cache_logf, state_conv, w_ada, b_ada, norm_g, w_in, b_f, q_norm_g, k_norm_g, conv_w, conv_b, conv_ln_g, conv_ln_b, w_oa, w_ob, w_out):
    batch, seq, d = x_prompt.shape
    dec_batch, dec_seq, _ = x_sample.shape
    depth = w_ada.shape[0]
    past = cache_k.shape[2]
    assert d == D_MODEL and seq % 1024 == 0 and (dec_batch * dec_seq) % 1024 == 0
    assert dec_seq >= HALO and seq >= HALO and dec_seq <= LANES

    nb = batch + dec_batch
    nb_pad = -(-nb // 8) * 8
    c_all = jnp.concatenate([c_prompt, c_sample, jnp.zeros((nb_pad - nb, d), F32)], axis=0)
    mod_all = _ada_call(c_all, w_ada, b_ada)

    xp = x_prompt.reshape(batch * seq, d)
    xs = x_sample.reshape(dec_batch * dec_seq, d)
    cache_k2 = cache_k.reshape(depth, dec_batch, past * N_HEADS, HEAD_DIM)
    cache_v2 = cache_v.reshape(depth, dec_batch, past * N_HEADS, HEAD_DIM)
    cache_lf_t = jnp.swapaxes(cache_logf, 2, 3)

    w_all = _prep_in_call(w_in)
    w_f = jnp.pad(w_in[:, :, FORGET_COL:FORGET_COL + N_HEADS],
                  ((0, 0), (0, 0), (0, LANES - N_HEADS))).astype(BF16)
    woa, wob, wout = _cast_call(w_oa), _cast_call(w_ob), _cast_call(w_out)
    wl = dict(
        norm_g=norm_g.reshape(depth, 1, d), w_all=w_all, w_f=w_f,
        b_f=jnp.pad(b_f, ((0, 0), (0, LANES - N_HEADS))).reshape(depth, 1, LANES),
        q_g=q_norm_g.reshape(depth, 1, HEAD_DIM), k_g=k_norm_g.reshape(depth, 1, HEAD_DIM),
    )
    cb = conv_b.reshape(depth, 1, D_CONV)
    lg, lb = conv_ln_g.reshape(depth, 1, D_CONV), conv_ln_b.reshape(depth, 1, D_CONV)

    conv_p, conv_s = [], []
    st_p = st_s = None
    for l in range(depth):
        lf, q, st_p, kb, vb, ga, u, gb, mab = _layer_group(xp, mod_all, seq, 0, wl, l, depth, st_p)
        f_col, f_row = _cum_p_call(lf, batch, seq)
        ya = _attn_p_call(q, kb, vb, f_col, f_row, ga, batch, seq)
        yb = _conv_call(u, u, gb, conv_w, cb, lg, lb, l, 512, seq // 512, True)
        xp = _out_call(ya, yb, mab, xp, mod_all, woa, wob, wout, l, seq, 0)
        conv_p.append(u.reshape(batch, seq, D_CONV)[:, seq - HALO:, :])

        lf, q, st_s, kb, vb, ga, u, gb, mab = _layer_group(xs, mod_all, dec_seq, batch, wl, l, depth, st_s)
        g_row, fn_col, fn_row = _cum_s_call(cache_lf_t, lf, l, dec_batch, past, dec_seq)
        ya = _attn_s_call(q, cache_k2, cache_v2, g_row, fn_col, fn_row, kb, vb, ga,
                          l, dec_batch, past, dec_seq)
        yb = _conv_call(u, state_conv, gb, conv_w, cb, lg, lb, l, dec_seq, 1, False)
        xs = _out_call(ya, yb, mab, xs, mod_all, woa, wob, wout, l, dec_seq, batch)
        conv_s.append(u.reshape(dec_batch, dec_seq, D_CONV)[:, dec_seq - HALO:, :])

    cache_p = (depth, batch, seq, N_HEADS, HEAD_DIM)
    cache_s = (depth, dec_batch, dec_seq, N_HEADS, HEAD_DIM)
    return (xp.reshape(batch, seq, d), xs.reshape(dec_batch, dec_seq, d),
            st_p[1].reshape(cache_p), st_p[2].reshape(cache_p),
            st_p[0].reshape(depth, batch, seq, N_HEADS), jnp.stack(conv_p),
            st_s[1].reshape(cache_s), st_s[2].reshape(cache_s),
            st_s[0].reshape(depth, dec_batch, dec_seq, N_HEADS), jnp.stack(conv_s))
```

```python
import functools

import jax
import jax.numpy as jnp
from jax import lax
from jax.experimental import pallas as pl
from jax.experimental.pallas import tpu as pltpu

D_MODEL = 2048
N_HEADS = 8
HEAD_DIM = 128
D_ATTN = N_HEADS * HEAD_DIM
D_CONV = 1024
CONV_WIDTH = 31
HALO = CONV_WIDTH - 1
EPS = 1e-6
NEG = -1e30
LOG2E = 1.4426950408889634

LANES = 128
HALO_PAD = 32
CUM_CHUNK = 256
MIB = 1024 * 1024

BF16 = jnp.bfloat16
F32 = jnp.float32


def _params(sem, vmem_mib):
    return pltpu.CompilerParams(dimension_semantics=sem, vmem_limit_bytes=vmem_mib * MIB)


def _layer_spec(shape, layer):
    nd = len(shape) - 1
    return pl.BlockSpec((None,) + tuple(shape[1:]), lambda *_: (layer,) + (0,) * nd, pipeline_mode=pl.Buffered(1))


def _sigmoid(x):
    return 1.0 / (1.0 + jnp.exp(-x))


def _silu(x):
    return x * _sigmoid(x)


def _log_sigmoid(x):
    return jnp.minimum(x, 0.0) - jnp.log1p(jnp.exp(-jnp.abs(x)))


def _nt_dot(a, b):
    return lax.dot_general(a, b, (((1,), (1,)), ((), ())), preferred_element_type=F32)


def _dot(a, b):
    return jnp.dot(a, b, preferred_element_type=F32)


def _split3(x):
    hi = x.astype(BF16)
    r1 = x - hi.astype(F32)
    mid = r1.astype(BF16)
    lo = (r1 - mid.astype(F32)).astype(BF16)
    return hi, mid, lo


def _ada_kernel(c_ref, w_ref, b_ref, o_ref):
    s = _silu(c_ref[...]).astype(BF16)
    o_ref[...] = _dot(s, w_ref[...].astype(BF16)) + b_ref[...]


def _ada_call(c_all, w_ada, b_ada):
    depth, d, e = w_ada.shape
    nb = c_all.shape[0]
    tn = 1024
    return pl.pallas_call(
        _ada_kernel,
        grid=(depth, e // tn),
        in_specs=[
            pl.BlockSpec((nb, d), lambda l, n: (0, 0)),
            pl.BlockSpec((None, d, tn), lambda l, n: (l, 0, n)),
            pl.BlockSpec((None, 1, tn), lambda l, n: (l, 0, n)),
        ],
        out_specs=pl.BlockSpec((None, nb, tn), lambda l, n: (l, 0, n)),
        out_shape=jax.ShapeDtypeStruct((depth, nb, e), F32),
        compiler_params=_params(("parallel", "parallel"), 32),
        name="ada_mod",
    )(c_all, w_ada, b_ada.reshape(depth, 1, e))


def _batch_of(row, rows_per_batch, batch0):
    return batch0 + lax.div(row, rows_per_batch)


def _norm_kernel(x_ref, mod_ref, g_ref, h_ref, *, tm, rows_per_batch, batch0):
    m = pl.program_id(0)
    sub = min(tm, rows_per_batch)
    for i in range(tm // sub):
        b = _batch_of(m * tm + i * sub, rows_per_batch, batch0)
        shift = mod_ref[pl.ds(b, 1), 0:D_MODEL]
        scale = mod_ref[pl.ds(b, 1), D_MODEL:2 * D_MODEL]
        x = x_ref[i * sub:(i + 1) * sub, :]
        ms = jnp.mean(x * x, axis=-1, keepdims=True)
        y = x * lax.rsqrt(ms + EPS) * g_ref[...]
        h_ref[i * sub:(i + 1) * sub, :] = (y * (1.0 + scale) + shift).astype(BF16)


def _norm_call(x, mod, g, layer, rows_per_batch, batch0):
    rows = x.shape[0]
    tm = 1024
    kern = functools.partial(_norm_kernel, tm=tm, rows_per_batch=rows_per_batch, batch0=batch0)
    return pl.pallas_call(
        kern,
        grid=(rows // tm,),
        in_specs=[
            pl.BlockSpec((tm, D_MODEL), lambda m: (m, 0)),
            _layer_spec(mod.shape, layer),
            _layer_spec(g.shape, layer),
        ],
        out_specs=pl.BlockSpec((tm, D_MODEL), lambda m: (m, 0)),
        out_shape=jax.ShapeDtypeStruct((rows, D_MODEL), BF16),
        compiler_params=_params(("parallel",), 32),
        name="norm_mod",
    )(x, mod, g)


def _head_rmsnorm(a, g):
    ms = jnp.mean(a * a, axis=-1, keepdims=True)
    return a * lax.rsqrt(ms + EPS) * g


def _store_heads(y, hd, tm, cache_ref, bf_ref):
    cache_ref[pl.ds(hd, tm, stride=N_HEADS), :] = y
    bf_ref[:, hd * HEAD_DIM:(hd + 1) * HEAD_DIM] = y.astype(BF16)


def _in_a_kernel(*refs, tm, n_alias):
    (h_ref, w_ref, wf_ref, bf_ref, qg_ref, kg_ref) = refs[:6]
    (lf_ref, lf8_ref, q_ref, k_ref, v_ref, kb_ref, vb_ref, ga_ref) = refs[6 + n_alias:]
    n = pl.program_id(1)

    def head(acc, hd):
        return acc[:, hd * HEAD_DIM:(hd + 1) * HEAD_DIM]

    @pl.when(n == 0)
    def _():
        f = _dot(h_ref[...], wf_ref[...]) + bf_ref[...]
        lf = _log_sigmoid(f)
        lf_ref[...] = lf
        lf8_ref[...] = lf[:, 0:N_HEADS]
        acc = _dot(h_ref[...], w_ref[...])
        for hd in range(N_HEADS):
            y = _head_rmsnorm(head(acc, hd), qg_ref[...]) * (HEAD_DIM ** -0.5 * LOG2E)
            q_ref[:, hd * HEAD_DIM:(hd + 1) * HEAD_DIM] = y.astype(BF16)

    @pl.when(n == 1)
    def _():
        acc = _dot(h_ref[...], w_ref[...])
        for hd in range(N_HEADS):
            _store_heads(_head_rmsnorm(head(acc, hd), kg_ref[...]), hd, tm, k_ref, kb_ref)

    @pl.when(n == 2)
    def _():
        acc = _dot(h_ref[...], w_ref[...])
        for hd in range(N_HEADS):
            _store_heads(head(acc, hd), hd, tm, v_ref, vb_ref)

    @pl.when(n == 3)
    def _():
        ga_ref[...] = _silu(_dot(h_ref[...], w_ref[...])).astype(BF16)


def _in_a_call(h, w_all, w_f, b_f, q_g, k_g, layer, depth, stacked_prev):
    rows = h.shape[0]
    tm, tn = 1024, 1024
    row_blk = lambda width: pl.BlockSpec((tm, width), lambda m, n: (m, 0))
    cache_blk = pl.BlockSpec((None, tm * N_HEADS, HEAD_DIM), lambda m, n: (layer, m, 0))
    cache_shape = jax.ShapeDtypeStruct((depth, rows * N_HEADS, HEAD_DIM), F32)
    alias_in = [] if stacked_prev is None else list(stacked_prev)
    n_in = 6
    return pl.pallas_call(
        functools.partial(_in_a_kernel, tm=tm, n_alias=len(alias_in)),
        grid=(rows // tm, 4),
        in_specs=[
            row_blk(D_MODEL),
            pl.BlockSpec((None, D_MODEL, tn), lambda m, n: (layer, 0, n)),
            _layer_spec(w_f.shape, layer),
            _layer_spec(b_f.shape, layer),
            _layer_spec(q_g.shape, layer),
            _layer_spec(k_g.shape, layer),
        ] + [pl.BlockSpec(memory_space=pl.ANY)] * len(alias_in),
        out_specs=[row_blk(LANES),
                   pl.BlockSpec((None, tm, N_HEADS), lambda m, n: (layer, m, 0)),
                   row_blk(D_ATTN), cache_blk, cache_blk,
                   row_blk(D_ATTN), row_blk(D_ATTN), row_blk(D_ATTN)],
        out_shape=[
            jax.ShapeDtypeStruct((rows, LANES), F32),
            jax.ShapeDtypeStruct((depth, rows, N_HEADS), F32),
            jax.ShapeDtypeStruct((rows, D_ATTN), BF16),
            cache_shape,
            cache_shape,
            jax.ShapeDtypeStruct((rows, D_ATTN), BF16),
            jax.ShapeDtypeStruct((rows, D_ATTN), BF16),
            jax.ShapeDtypeStruct((rows, D_ATTN), BF16),
        ],
        input_output_aliases={n_in + i: out for i, out in zip(range(len(alias_in)), (1, 3, 4))},
        compiler_params=_params(("parallel", "arbitrary"), 58),
        name="in_proj_a",
    )(h, w_all, w_f, b_f, q_g, k_g, *alias_in)


IN_A_TILES = 4


def _in_b_kernel(h_ref, w_ref, u_ref, gb_ref, mab_ref, lin_sc):
    n = pl.program_id(1)

    @pl.when(n == 0)
    def _():
        lin_sc[...] = _dot(h_ref[...], w_ref[...])

    @pl.when(n == 1)
    def _():
        u_ref[...] = lin_sc[...] * _sigmoid(_dot(h_ref[...], w_ref[...]))

    @pl.when(n == 2)
    def _():
        gb_ref[...] = _silu(_dot(h_ref[...], w_ref[...])).astype(BF16)

    @pl.when(n > 2)
    def _():
        mab_ref[...] = _sigmoid(_dot(h_ref[...], w_ref[...])).astype(BF16)


def _in_b_call(h, w_all, layer):
    rows = h.shape[0]
    tm, tn = 1024, 1024
    return pl.pallas_call(
        _in_b_kernel,
        grid=(rows // tm, 7),
        in_specs=[
            pl.BlockSpec((tm, D_MODEL), lambda m, n: (m, 0)),
            pl.BlockSpec((None, D_MODEL, tn), lambda m, n: (layer, 0, n + IN_A_TILES)),
        ],
        out_specs=[
            pl.BlockSpec((tm, D_CONV), lambda m, n: (m, 0)),
            pl.BlockSpec((tm, D_CONV), lambda m, n: (m, 0)),
            pl.BlockSpec((tm, tn), lambda m, n: (m, jnp.clip(n - 3, 0, 3))),
        ],
        out_shape=[
            jax.ShapeDtypeStruct((rows, D_CONV), F32),
            jax.ShapeDtypeStruct((rows, D_CONV), BF16),
            jax.ShapeDtypeStruct((rows, 2 * D_MODEL), BF16),
        ],
        scratch_shapes=[pltpu.VMEM((tm, D_CONV), F32)],
        compiler_params=_params(("parallel", "arbitrary"), 48),
        name="in_proj_b",
    )(h, w_all)


def _tri(n, upper):
    r = lax.broadcasted_iota(jnp.int32, (n, n), 0)
    c = lax.broadcasted_iota(jnp.int32, (n, n), 1)
    keep = (r <= c) if upper else (c <= r)
    return jnp.where(keep, 1.0, 0.0).astype(BF16)


def _cum_p_kernel(lf_ref, fc_ref, fr_ref, *, seq):
    tril = _tri(CUM_CHUNK, upper=False)
    carry = jnp.zeros((1, LANES), F32)
    for i in range(seq // CUM_CHUNK):
        sl = slice(i * CUM_CHUNK, (i + 1) * CUM_CHUNK)
        hi, mid, lo = _split3(lf_ref[sl, :])
        y = _dot(tril, hi) + _dot(tril, mid) + _dot(tril, lo) + carry
        carry = y[CUM_CHUNK - 1:CUM_CHUNK, :]
        y2 = y * LOG2E
        fc_ref[sl, :] = y2
        fr_ref[:, sl] = y2.T[0:N_HEADS, :]


def _cum_p_call(lf, batch, seq):
    kern = functools.partial(_cum_p_kernel, seq=seq)
    return pl.pallas_call(
        kern,
        grid=(batch,),
        in_specs=[pl.BlockSpec((seq, LANES), lambda b: (b, 0))],
        out_specs=[
            pl.BlockSpec((seq, LANES), lambda b: (b, 0)),
            pl.BlockSpec((None, N_HEADS, seq), lambda b: (b, 0, 0)),
        ],
        out_shape=[
            jax.ShapeDtypeStruct((batch * seq, LANES), F32),
            jax.ShapeDtypeStruct((batch, N_HEADS, seq), F32),
        ],
        compiler_params=_params(("parallel",), 32),
        name="cum_prompt",
    )(lf)


def _cum_s_kernel(clf_ref, lfn_ref, g_ref, fnc_ref, fnr_ref, f_sc, *, past, new):
    triu = _tri(CUM_CHUNK, upper=True)
    carry = jnp.zeros((N_HEADS, 1), F32)
    for i in range(past // CUM_CHUNK):
        sl = slice(i * CUM_CHUNK, (i + 1) * CUM_CHUNK)
        hi, mid, lo = _split3(clf_ref[:, sl])
        y = _dot(hi, triu) + _dot(mid, triu) + _dot(lo, triu) + carry
        carry = y[:, CUM_CHUNK - 1:CUM_CHUNK]
        f_sc[:, sl] = y
    g_ref[...] = (carry - f_sc[...]) * LOG2E
    tril = _tri(LANES, upper=False)
    xn = jnp.concatenate([lfn_ref[...], jnp.zeros((LANES - new, LANES), F32)], axis=0)
    hi, mid, lo = _split3(xn)
    yn = (_dot(tril, hi) + _dot(tril, mid) + _dot(tril, lo)) * LOG2E
    fnc_ref[...] = yn[0:new, :]
    fnr_ref[...] = yn.T[0:N_HEADS, :]


def _cum_s_call(clf_t, lf_new, layer, batch, past, new):
    kern = functools.partial(_cum_s_kernel, past=past, new=new)
    return pl.pallas_call(
        kern,
        grid=(batch,),
        in_specs=[
            pl.BlockSpec((None, None, N_HEADS, past), lambda b: (layer, b, 0, 0)),
            pl.BlockSpec((new, LANES), lambda b: (b, 0)),
        ],
        out_specs=[
            pl.BlockSpec((None, N_HEADS, past), lambda b: (b, 0, 0)),
            pl.BlockSpec((new, LANES), lambda b: (b, 0)),
            pl.BlockSpec((None, N_HEADS, LANES), lambda b: (b, 0, 0)),
        ],
        out_shape=[
            jax.ShapeDtypeStruct((batch, N_HEADS, past), F32),
            jax.ShapeDtypeStruct((batch * new, LANES), F32),
            jax.ShapeDtypeStruct((batch, N_HEADS, LANES), F32),
        ],
        scratch_shapes=[pltpu.VMEM((N_HEADS, past), F32)],
        compiler_params=_params(("parallel",), 32),
        name="cum_sample",
    )(clf_t, lf_new)


def _softmax_parts(carry, t, fq):
    m, l, acc = carry
    m_new = jnp.maximum(m, jnp.max(t, axis=-1, keepdims=True) + fq)
    alpha = jnp.exp2(m - m_new)
    p = jnp.exp2(t + (fq - m_new))
    return m_new, alpha * l + jnp.sum(p, axis=-1, keepdims=True), alpha * acc, p.astype(BF16)


def _softmax_step(carry, t, fq, v):
    m_new, l, acc, p = _softmax_parts(carry, t, fq)
    return m_new, l, acc + _dot(p, v)


def _causal(s):
    row = lax.broadcasted_iota(jnp.int32, s.shape, 0)
    col = lax.broadcasted_iota(jnp.int32, s.shape, 1)
    return jnp.where(col <= row, s, NEG)


def _attn_p_kernel(q_ref, k_ref, v_ref, fc_ref, fr_ref, ga_ref, o_ref, *, seq, tq):
    hd = pl.program_id(1)
    nq = seq // tq
    lane = lax.broadcasted_iota(jnp.int32, (seq, LANES), 1)
    fq_all = jnp.sum(jnp.where(lane == hd, fc_ref[...], 0.0), axis=-1, keepdims=True)
    fr = fr_ref[pl.ds(hd, 1), :]
    blk = lambda i: slice(i * tq, (i + 1) * tq)

    def scores(i, j):
        t = _nt_dot(q_ref[blk(i), :], k_ref[blk(j), :]) - fr[:, blk(j)]
        return _causal(t) if i == j else t

    pairs = [(i, j) for i in range(nq) for j in range(i + 1)]
    t_next = scores(*pairs[0])
    carry = None
    for n, (i, j) in enumerate(pairs):
        t = t_next
        if n + 1 < len(pairs):
            t_next = scores(*pairs[n + 1])
        if j == 0:
            carry = (jnp.full((tq, 1), NEG, F32), jnp.zeros((tq, 1), F32), jnp.zeros((tq, HEAD_DIM), F32))
        carry = _softmax_step(carry, t, fq_all[blk(i), :], v_ref[blk(j), :])
        if j == i:
            _, l, acc = carry
            o_ref[blk(i), :] = (acc / l * ga_ref[blk(i), :].astype(F32)).astype(BF16)


def _attn_p_call(q, k, v, f_col, f_row, ga, batch, seq):
    tq = 512
    head = pl.BlockSpec((seq, HEAD_DIM), lambda b, h: (b, h))
    return pl.pallas_call(
        functools.partial(_attn_p_kernel, seq=seq, tq=tq),
        grid=(batch, N_HEADS),
        in_specs=[
            head, head, head,
            pl.BlockSpec((seq, LANES), lambda b, h: (b, 0)),
            pl.BlockSpec((None, N_HEADS, seq), lambda b, h: (b, 0, 0)),
            head,
        ],
        out_specs=head,
        out_shape=jax.ShapeDtypeStruct((batch * seq, D_ATTN), BF16),
        compiler_params=_params(("parallel", "parallel"), 32),
        name="attn_prompt",
    )(q, k, v, f_col, f_row, ga)


def _attn_s_kernel(q_ref, ck_ref, cv_ref, g_ref, fnc_ref, fnr_ref, kn_ref, vn_ref, ga_ref,
                   o_ref, m_sc, l_sc, acc_sc, *, new, tk):
    j = pl.program_id(1)
    last = pl.num_programs(1) - 1

    @pl.when(j == 0)
    def _():
        m_sc[...] = jnp.full(m_sc.shape, NEG, F32)
        l_sc[...] = jnp.zeros(l_sc.shape, F32)
        acc_sc[...] = jnp.zeros(acc_sc.shape, F32)

    heads = range(N_HEADS)
    sls = [slice(hd * HEAD_DIM, (hd + 1) * HEAD_DIM) for hd in heads]
    state = [(m_sc[hd][:, 0:1], l_sc[hd][:, 0:1], acc_sc[:, sls[hd]]) for hd in heads]
    fq = [fnc_ref[:, hd:hd + 1] for hd in heads]

    ts = [_nt_dot(q_ref[:, sls[hd]], ck_ref[pl.ds(hd, tk, stride=N_HEADS), :].astype(BF16)) + g_ref[hd:hd + 1, :]
          for hd in heads]
    parts = [_softmax_parts(state[hd], ts[hd], fq[hd]) for hd in heads]
    out = []
    for hd in heads:
        m, l, acc, p = parts[hd]
        out.append((m, l, acc + _dot(p, cv_ref[pl.ds(hd, tk, stride=N_HEADS), :].astype(BF16))))
    for hd in heads:
        m, l, acc = out[hd]
        m_sc[hd] = jnp.broadcast_to(m, (new, LANES))
        l_sc[hd] = jnp.broadcast_to(l, (new, LANES))
        acc_sc[:, sls[hd]] = acc

    @pl.when(j == last)
    def _():
        pad = jnp.zeros((LANES - new, HEAD_DIM), BF16)
        ts = [_causal(_nt_dot(q_ref[:, sls[hd]], jnp.concatenate([kn_ref[:, sls[hd]], pad], axis=0))
                      - fnr_ref[hd:hd + 1, :]) for hd in heads]
        parts = [_softmax_parts(out[hd], ts[hd], fq[hd]) for hd in heads]
        for hd in heads:
            _, l, acc, p = parts[hd]
            acc = acc + _dot(p, jnp.concatenate([vn_ref[:, sls[hd]], pad], axis=0))
            o_ref[:, sls[hd]] = (acc / l * ga_ref[:, sls[hd]].astype(F32)).astype(BF16)


def _attn_s_call(q, cache_k, cache_v, g_row, fn_col, fn_row, k_new, v_new, ga, layer, batch, past, new):
    tk = 2048
    row = lambda width: pl.BlockSpec((new, width), lambda b, j: (b, 0))
    cache = pl.BlockSpec((None, None, tk * N_HEADS, HEAD_DIM), lambda b, j: (layer, b, j, 0))
    return pl.pallas_call(
        functools.partial(_attn_s_kernel, new=new, tk=tk),
        grid=(batch, past // tk),
        in_specs=[
            row(D_ATTN), cache, cache,
            pl.BlockSpec((None, N_HEADS, tk), lambda b, j: (b, 0, j)),
            row(LANES),
            pl.BlockSpec((None, N_HEADS, LANES), lambda b, j: (b, 0, 0)),
            row(D_ATTN), row(D_ATTN), row(D_ATTN),
        ],
        out_specs=row(D_ATTN),
        out_shape=jax.ShapeDtypeStruct((batch * new, D_ATTN), BF16),
        scratch_shapes=[
            pltpu.VMEM((N_HEADS, new, LANES), F32),
            pltpu.VMEM((N_HEADS, new, LANES), F32),
            pltpu.VMEM((new, D_ATTN), F32),
        ],
        compiler_params=_params(("parallel", "arbitrary"), 52),
        name="attn_sample",
    )(q, cache_k, cache_v, g_row, fn_col, fn_row, k_new, v_new, ga)


CONV_ROWS = 64


def _conv_kernel(u_ref, halo_ref, gb_ref, w_ref, cb_ref, lg_ref, lb_ref, o_ref, ext_sc, cv_sc,
                 *, tt, tiles_per_batch, halo_from_u):
    nc = D_CONV // LANES
    chunk = lambda c: slice(c * LANES, (c + 1) * LANES)
    for c in range(nc):
        ext_sc[c, HALO_PAD:HALO_PAD + tt, :] = u_ref[:, chunk(c)]
    if halo_from_u:
        first = lax.rem(pl.program_id(0), tiles_per_batch) == 0

        @pl.when(first)
        def _():
            ext_sc[:, 0:HALO_PAD, :] = jnp.zeros((nc, HALO_PAD, LANES), F32)

        @pl.when(jnp.logical_not(first))
        def _():
            for c in range(nc):
                ext_sc[c, 0:HALO_PAD, :] = halo_ref[:, chunk(c)]
    else:
        for c in range(nc):
            ext_sc[c, HALO_PAD - HALO:HALO_PAD, :] = halo_ref[:, chunk(c)]

    base = HALO_PAD - HALO
    for c in range(nc):
        w_taps = w_ref[:, chunk(c)]
        bias = cb_ref[:, chunk(c)]

        def rows_body(r, carry, c=c, w_taps=w_taps, bias=bias):
            r0 = pl.multiple_of(r * CONV_ROWS, CONV_ROWS)
            acc = jnp.zeros((CONV_ROWS, LANES), F32)
            for j in range(CONV_WIDTH):
                acc = acc + w_taps[j:j + 1, :] * ext_sc[c, pl.ds(r0 + (base + j), CONV_ROWS, stride=1), :]
            cv_sc[pl.ds(r0, CONV_ROWS), chunk(c)] = acc + bias
            return carry

        lax.fori_loop(0, tt // CONV_ROWS, rows_body, 0)

    cv = cv_sc[...]
    mu = jnp.mean(cv, axis=-1, keepdims=True)
    d = cv - mu
    var = jnp.mean(d * d, axis=-1, keepdims=True)
    y = d * lax.rsqrt(var + EPS) * lg_ref[...] + lb_ref[...]
    o_ref[...] = (_silu(y) * gb_ref[...].astype(F32)).astype(BF16)


def _conv_call(u, halo, gb, conv_w, conv_b, ln_g, ln_b, layer, tt, tiles_per_batch, halo_from_u):
    rows = u.shape[0]
    tile = pl.BlockSpec((tt, D_CONV), lambda t: (t, 0))
    if halo_from_u:
        per = tt // HALO_PAD
        halo_spec = pl.BlockSpec((HALO_PAD, D_CONV), lambda t: (jnp.maximum(t * per - 1, 0), 0))
    else:
        halo_spec = pl.BlockSpec((None, None, HALO, D_CONV), lambda t: (layer, t, 0, 0))
    kern = functools.partial(_conv_kernel, tt=tt, tiles_per_batch=tiles_per_batch, halo_from_u=halo_from_u)
    return pl.pallas_call(
        kern,
        grid=(rows // tt,),
        in_specs=[
            tile, halo_spec, tile,
            _layer_spec(conv_w.shape, layer), _layer_spec(conv_b.shape, layer),
            _layer_spec(ln_g.shape, layer), _layer_spec(ln_b.shape, layer),
        ],
        out_specs=tile,
        out_shape=jax.ShapeDtypeStruct((rows, D_CONV), BF16),
        scratch_shapes=[
            pltpu.VMEM((D_CONV // LANES, HALO_PAD + tt, LANES), F32),
            pltpu.VMEM((tt, D_CONV), F32),
        ],
        compiler_params=_params(("parallel",), 32),
        name="conv_module",
    )(u, halo, gb, conv_w, conv_b, ln_g, ln_b)


OUT_COLS = 512


def _out_kernel(ya_ref, yb_ref, mab_ref, x_ref, mod_ref, woa_ref, wob_ref, wout_ref, o_ref, mg_sc,
                *, tm, rows_per_batch, batch0):
    m = pl.program_id(0)
    ya = ya_ref[...]
    yb = yb_ref[...]
    for c in range(D_MODEL // OUT_COLS):
        cs = slice(c * OUT_COLS, (c + 1) * OUT_COLS)
        cs_b = slice(D_MODEL + c * OUT_COLS, D_MODEL + (c + 1) * OUT_COLS)
        pa = _dot(ya, woa_ref[:, cs])
        pb = _dot(yb, wob_ref[:, cs])
        merged = mab_ref[:, cs].astype(F32) * pa + mab_ref[:, cs_b].astype(F32) * pb
        mg_sc[:, cs] = merged.astype(BF16)
    out = _dot(mg_sc[...], wout_ref[...])
    sub = min(tm, rows_per_batch)
    for i in range(tm // sub):
        b = _batch_of(m * tm + i * sub, rows_per_batch, batch0)
        gate = mod_ref[pl.ds(b, 1), 2 * D_MODEL:3 * D_MODEL]
        rs = slice(i * sub, (i + 1) * sub)
        o_ref[rs, :] = x_ref[rs, :] + gate * out[rs, :]


def _out_call(ya, yb, mab, x, mod, w_oa, w_ob, w_out, layer, rows_per_batch, batch0):
    rows = x.shape[0]
    tm = 512
    row_blk = lambda width: pl.BlockSpec((tm, width), lambda m: (m, 0))
    kern = functools.partial(_out_kernel, tm=tm, rows_per_batch=rows_per_batch, batch0=batch0)
    return pl.pallas_call(
        kern,
        grid=(rows // tm,),
        in_specs=[
            row_blk(D_ATTN), row_blk(D_CONV), row_blk(2 * D_MODEL), row_blk(D_MODEL),
            _layer_spec(mod.shape, layer), _layer_spec(w_oa.shape, layer),
            _layer_spec(w_ob.shape, layer), _layer_spec(w_out.shape, layer),
        ],
        out_specs=row_blk(D_MODEL),
        out_shape=jax.ShapeDtypeStruct((rows, D_MODEL), F32),
        scratch_shapes=[pltpu.VMEM((tm, D_MODEL), BF16)],
        compiler_params=_params(("parallel",), 56),
        name="out_proj",
    )(ya, yb, mab, x, mod, w_oa, w_ob, w_out)


PREP_TILE = 1024
FORGET_COL = 3 * D_ATTN


def _prep_in_kernel(a_ref, b_ref, o_ref):
    i = pl.program_id(1)

    @pl.when(i < FORGET_COL // PREP_TILE)
    def _():
        o_ref[...] = a_ref[...]

    @pl.when(i >= FORGET_COL // PREP_TILE)
    def _():
        o_ref[...] = jnp.concatenate([a_ref[:, N_HEADS:], b_ref[:, :N_HEADS]], axis=1)


def _prep_in_call(w_in):
    depth, d, d_in = w_in.shape
    n_tiles = (d_in - N_HEADS) // PREP_TILE
    first = FORGET_COL // PREP_TILE
    starts = [(i + 1) * PREP_TILE for i in range(first, n_tiles)]
    tails = jnp.stack([w_in[:, :, s:s + N_HEADS] for s in starts], axis=1)
    tails = jnp.pad(tails, ((0, 0), (0, 0), (0, 0), (0, LANES - N_HEADS)))
    rows = d // 2
    return pl.pallas_call(
        _prep_in_kernel,
        grid=(depth, n_tiles, d // rows),
        in_specs=[
            pl.BlockSpec((None, rows, PREP_TILE), lambda l, i, r: (l, r, i)),
            pl.BlockSpec((None, None, rows, LANES), lambda l, i, r: (l, jnp.maximum(i - first, 0), r, 0)),
        ],
        out_specs=pl.BlockSpec((None, rows, PREP_TILE), lambda l, i, r: (l, r, i)),
        out_shape=jax.ShapeDtypeStruct((depth, d, n_tiles * PREP_TILE), BF16),
        compiler_params=_params(("parallel", "parallel", "parallel"), 32),
        name="prep_w_in",
    )(w_in, tails)


def _cast_kernel(x_ref, o_ref):
    o_ref[...] = x_ref[...].astype(BF16)


def _cast_call(w):
    depth, k, n = w.shape
    rows = 512
    blk = pl.BlockSpec((None, rows, n), lambda l, r: (l, r, 0))
    return pl.pallas_call(
        _cast_kernel,
        grid=(depth, k // rows),
        in_specs=[blk],
        out_specs=blk,
        out_shape=jax.ShapeDtypeStruct(w.shape, BF16),
        compiler_params=_params(("parallel", "parallel"), 32),
        name="cast_bf16",
    )(w)


def _layer_group(x, mod, rows_per_batch, batch0, wl, layer, depth, stacked_prev):
    h = _norm_call(x, mod, wl["norm_g"], layer, rows_per_batch, batch0)
    lf, lf_all, q, k_all, v_all, kb, vb, ga = _in_a_call(
        h, wl["w_all"], wl["w_f"], wl["b_f"], wl["q_g"], wl["k_g"], layer, depth, stacked_prev)
    u, gb, mab = _in_b_call(h, wl["w_all"], layer)
    return lf, q, (lf_all, k_all, v_all), kb, vb, ga, u, gb, mab


def kernel(x_prompt, x_sample, c_prompt, c_sample, cache_k, cache_v, cache_logf, state_conv, w_ada, b_ada, norm_g, w_in, b_f, q_norm_g, k_norm_g, conv_w, conv_b, conv_ln_g, conv_ln_b, w_oa, w_ob, w_out):
    batch, seq, d = x_prompt.shape
    dec_batch, dec_seq, _ = x_sample.shape
    depth = w_ada.shape[0]
    past = cache_k.shape[2]
    assert d == D_MODEL and seq % 1024 == 0 and (dec_batch * dec_seq) % 1024 == 0
    assert dec_seq >= HALO and seq >= HALO and dec_seq <= LANES

    nb = batch + dec_batch
    nb_pad = -(-nb // 8) * 8
    c_all = jnp.concatenate([c_prompt, c_sample, jnp.zeros((nb_pad - nb, d), F32)], axis=0)
    mod_all = _ada_call(c_all, w_ada, b_ada)

    xp = x_prompt.reshape(batch * seq, d)
    xs = x_sample.reshape(dec_batch * dec_seq, d)
    cache_k2 = cache_k.reshape(depth, dec_batch, past * N_HEADS, HEAD_DIM)
    cache_v2 = cache_v.reshape(depth, dec_batch, past * N_HEADS, HEAD_DIM)
    cache_lf_t = jnp.swapaxes(cache_logf, 2, 3)

    w_in_bf = w_in.astype(BF16)
    w_all = _prep_in_call(w_in_bf)
    w_f = jnp.pad(w_in_bf[:, :, FORGET_COL:FORGET_COL + N_HEADS], ((0, 0), (0, 0), (0, LANES - N_HEADS)))
    woa, wob, wout = _cast_call(w_oa), _cast_call(w_ob), _cast_call(w_out)
    wl = dict(
        norm_g=norm_g.reshape(depth, 1, d), w_all=w_all, w_f=w_f,
        b_f=jnp.pad(b_f, ((0, 0), (0, LANES - N_HEADS))).reshape(depth, 1, LANES),
        q_g=q_norm_g.reshape(depth, 1, HEAD_DIM), k_g=k_norm_g.reshape(depth, 1, HEAD_DIM),
    )
    cb = conv_b.reshape(depth, 1, D_CONV)
    lg, lb = conv_ln_g.reshape(depth, 1, D_CONV), conv_ln_b.reshape(depth, 1, D_CONV)

    conv_p, conv_s = [], []
    st_p = st_s = None
    for l in range(depth):
        lf, q, st_p, kb, vb, ga, u, gb, mab = _layer_group(xp, mod_all, seq, 0, wl, l, depth, st_p)
        f_col, f_row = _cum_p_call(lf, batch, seq)
        ya = _attn_p_call(q, kb, vb, f_col, f_row, ga, batch, seq)
        yb = _conv_call(u, u, gb, conv_w, cb, lg, lb, l, 512, seq // 512, True)
        xp = _out_call(ya, yb, mab, xp, mod_all, woa, wob, wout, l, seq, 0)
        conv_p.append(u.reshape(batch, seq, D_CONV)[:, seq - HALO:, :])

        lf, q, st_s, kb, vb, ga, u, gb, mab = _layer_group(xs, mod_all, dec_seq, batch, wl, l, depth, st_s)
        g_row, fn_col, fn_row = _cum_s_call(cache_lf_t, lf, l, dec_batch, past, dec_seq)
        ya = _attn_s_call(q, cache_k2, cache_v2, g_row, fn_col, fn_row, kb, vb, ga,
                          l, dec_batch, past, dec_seq)
        yb = _conv_call(u, state_conv, gb, conv_w, cb, lg, lb, l, dec_seq, 1, False)
        xs = _out_call(ya, yb, mab, xs, mod_all, woa, wob, wout, l, dec_seq, batch)
        conv_s.append(u.reshape(dec_batch, dec_seq, D_CONV)[:, dec_seq - HALO:, :])

    cache_p = (depth, batch, seq, N_HEADS, HEAD_DIM)
    cache_s = (depth, dec_batch, dec_seq, N_HEADS, HEAD_DIM)
    return (xp.reshape(batch, seq, d), xs.reshape(dec_batch, dec_seq, d),
            st_p[1].reshape(cache_p), st_p[2].reshape(cache_p),
            st_p[0].reshape(depth, batch, seq, N_HEADS), jnp.stack(conv_p),
            st_s[1].reshape(cache_s), st_s[2].reshape(cache_s),
            st_s[0].reshape(depth, dec_batch, dec_seq, N_HEADS), jnp.stack(conv_s))
```

```python
import functools

import jax
import jax.numpy as jnp
from jax import lax
from jax.experimental import pallas as pl
from jax.experimental.pallas import tpu as pltpu

D_MODEL = 2048
N_HEADS = 8
HEAD_DIM = 128
D_ATTN = N_HEADS * HEAD_DIM
D_CONV = 1024
CONV_WIDTH = 31
HALO = CONV_WIDTH - 1
EPS = 1e-6
NEG = -1e30
LOG2E = 1.4426950408889634

LANES = 128
HALO_PAD = 32
CUM_CHUNK = 256
MIB = 1024 * 1024

BF16 = jnp.bfloat16
F32 = jnp.float32


def _params(sem, vmem_mib):
    return pltpu.CompilerParams(dimension_semantics=sem, vmem_limit_bytes=vmem_mib * MIB)


def _layer_spec(shape, layer):
    nd = len(shape) - 1
    return pl.BlockSpec((None,) + tuple(shape[1:]), lambda *_: (layer,) + (0,) * nd, pipeline_mode=pl.Buffered(1))


def _sigmoid(x):
    return 1.0 / (1.0 + jnp.exp(-x))


def _silu(x):
    return x * _sigmoid(x)


def _log_sigmoid(x):
    return jnp.minimum(x, 0.0) - jnp.log1p(jnp.exp(-jnp.abs(x)))


def _nt_dot(a, b):
    return lax.dot_general(a, b, (((1,), (1,)), ((), ())), preferred_element_type=F32)


def _dot(a, b):
    return jnp.dot(a, b, preferred_element_type=F32)


def _split3(x):
    hi = x.astype(BF16)
    r1 = x - hi.astype(F32)
    mid = r1.astype(BF16)
    lo = (r1 - mid.astype(F32)).astype(BF16)
    return hi, mid, lo


def _ada_kernel(c_ref, w_ref, b_ref, o_ref):
    s = _silu(c_ref[...]).astype(BF16)
    o_ref[...] = _dot(s, w_ref[...].astype(BF16)) + b_ref[...]


def _ada_call(c_all, w_ada, b_ada):
    depth, d, e = w_ada.shape
    nb = c_all.shape[0]
    tn = 1024
    return pl.pallas_call(
        _ada_kernel,
        grid=(depth, e // tn),
        in_specs=[
            pl.BlockSpec((nb, d), lambda l, n: (0, 0)),
            pl.BlockSpec((None, d, tn), lambda l, n: (l, 0, n)),
            pl.BlockSpec((None, 1, tn), lambda l, n: (l, 0, n)),
        ],
        out_specs=pl.BlockSpec((None, nb, tn), lambda l, n: (l, 0, n)),
        out_shape=jax.ShapeDtypeStruct((depth, nb, e), F32),
        compiler_params=_params(("parallel", "parallel"), 32),
        name="ada_mod",
    )(c_all, w_ada, b_ada.reshape(depth, 1, e))


def _batch_of(row, rows_per_batch, batch0):
    return batch0 + lax.div(row, rows_per_batch)


def _norm_kernel(x_ref, mod_ref, g_ref, h_ref, *, tm, rows_per_batch, batch0):
    m = pl.program_id(0)
    sub = min(tm, rows_per_batch)
    for i in range(tm // sub):
        b = _batch_of(m * tm + i * sub, rows_per_batch, batch0)
        shift = mod_ref[pl.ds(b, 1), 0:D_MODEL]
        scale = mod_ref[pl.ds(b, 1), D_MODEL:2 * D_MODEL]
        x = x_ref[i * sub:(i + 1) * sub, :]
        ms = jnp.mean(x * x, axis=-1, keepdims=True)
        y = x * lax.rsqrt(ms + EPS) * g_ref[...]
        h_ref[i * sub:(i + 1) * sub, :] = (y * (1.0 + scale) + shift).astype(BF16)


def _norm_call(x, mod, g, layer, rows_per_batch, batch0):
    rows = x.shape[0]
    tm = 1024
    kern = functools.partial(_norm_kernel, tm=tm, rows_per_batch=rows_per_batch, batch0=batch0)
    return pl.pallas_call(
        kern,
        grid=(rows // tm,),
        in_specs=[
            pl.BlockSpec((tm, D_MODEL), lambda m: (m, 0)),
            _layer_spec(mod.shape, layer),
            _layer_spec(g.shape, layer),
        ],
        out_specs=pl.BlockSpec((tm, D_MODEL), lambda m: (m, 0)),
        out_shape=jax.ShapeDtypeStruct((rows, D_MODEL), BF16),
        compiler_params=_params(("parallel",), 32),
        name="norm_mod",
    )(x, mod, g)


def _head_rmsnorm(a, g):
    ms = jnp.mean(a * a, axis=-1, keepdims=True)
    return a * lax.rsqrt(ms + EPS) * g


def _store_heads(y, hd, tm, cache_ref, bf_ref):
    cache_ref[pl.ds(hd, tm, stride=N_HEADS), :] = y
    bf_ref[:, hd * HEAD_DIM:(hd + 1) * HEAD_DIM] = y.astype(BF16)


def _in_a_kernel(*refs, tm, n_alias):
    (h_ref, w_ref, wf_ref, bf_ref, qg_ref, kg_ref) = refs[:6]
    (lf_ref, lf8_ref, q_ref, k_ref, v_ref, kb_ref, vb_ref, ga_ref) = refs[6 + n_alias:]
    n = pl.program_id(1)

    def head(acc, hd):
        return acc[:, hd * HEAD_DIM:(hd + 1) * HEAD_DIM]

    @pl.when(n == 0)
    def _():
        f = _nt_dot(h_ref[...], wf_ref[...]) + bf_ref[...]
        lf = _log_sigmoid(f)
        lf_ref[...] = lf
        lf8_ref[...] = lf[:, 0:N_HEADS]
        acc = _nt_dot(h_ref[...], w_ref[...])
        for hd in range(N_HEADS):
            y = _head_rmsnorm(head(acc, hd), qg_ref[...]) * (HEAD_DIM ** -0.5 * LOG2E)
            q_ref[:, hd * HEAD_DIM:(hd + 1) * HEAD_DIM] = y.astype(BF16)

    @pl.when(n == 1)
    def _():
        acc = _nt_dot(h_ref[...], w_ref[...])
        for hd in range(N_HEADS):
            _store_heads(_head_rmsnorm(head(acc, hd), kg_ref[...]), hd, tm, k_ref, kb_ref)

    @pl.when(n == 2)
    def _():
        acc = _nt_dot(h_ref[...], w_ref[...])
        for hd in range(N_HEADS):
            _store_heads(head(acc, hd), hd, tm, v_ref, vb_ref)

    @pl.when(n == 3)
    def _():
        ga_ref[...] = _silu(_nt_dot(h_ref[...], w_ref[...])).astype(BF16)


def _in_a_call(h, w_all, w_f, b_f, q_g, k_g, layer, depth, stacked_prev):
    rows = h.shape[0]
    tm, tn = 1024, 1024
    row_blk = lambda width: pl.BlockSpec((tm, width), lambda m, n: (m, 0))
    cache_blk = pl.BlockSpec((None, tm * N_HEADS, HEAD_DIM), lambda m, n: (layer, m, 0))
    cache_shape = jax.ShapeDtypeStruct((depth, rows * N_HEADS, HEAD_DIM), F32)
    alias_in = [] if stacked_prev is None else list(stacked_prev)
    n_in = 6
    return pl.pallas_call(
        functools.partial(_in_a_kernel, tm=tm, n_alias=len(alias_in)),
        grid=(rows // tm, 4),
        in_specs=[
            row_blk(D_MODEL),
            pl.BlockSpec((None, tn, D_MODEL), lambda m, n: (layer, n, 0)),
            _layer_spec(w_f.shape, layer),
            _layer_spec(b_f.shape, layer),
            _layer_spec(q_g.shape, layer),
            _layer_spec(k_g.shape, layer),
        ] + [pl.BlockSpec(memory_space=pl.ANY)] * len(alias_in),
        out_specs=[row_blk(LANES),
                   pl.BlockSpec((None, tm, N_HEADS), lambda m, n: (layer, m, 0)),
                   row_blk(D_ATTN), cache_blk, cache_blk,
                   row_blk(D_ATTN), row_blk(D_ATTN), row_blk(D_ATTN)],
        out_shape=[
            jax.ShapeDtypeStruct((rows, LANES), F32),
            jax.ShapeDtypeStruct((depth, rows, N_HEADS), F32),
            jax.ShapeDtypeStruct((rows, D_ATTN), BF16),
            cache_shape,
            cache_shape,
            jax.ShapeDtypeStruct((rows, D_ATTN), BF16),
            jax.ShapeDtypeStruct((rows, D_ATTN), BF16),
            jax.ShapeDtypeStruct((rows, D_ATTN), BF16),
        ],
        input_output_aliases={n_in + i: out for i, out in zip(range(len(alias_in)), (1, 3, 4))},
        compiler_params=_params(("parallel", "arbitrary"), 58),
        name="in_proj_a",
    )(h, w_all, w_f, b_f, q_g, k_g, *alias_in)


IN_A_TILES = 4


def _in_b_kernel(h_ref, w_ref, u_ref, gb_ref, mab_ref, lin_sc):
    n = pl.program_id(1)

    @pl.when(n == 0)
    def _():
        lin_sc[...] = _nt_dot(h_ref[...], w_ref[...])

    @pl.when(n == 1)
    def _():
        u_ref[...] = lin_sc[...] * _sigmoid(_nt_dot(h_ref[...], w_ref[...]))

    @pl.when(n == 2)
    def _():
        gb_ref[...] = _silu(_nt_dot(h_ref[...], w_ref[...])).astype(BF16)

    @pl.when(n > 2)
    def _():
        mab_ref[...] = _sigmoid(_nt_dot(h_ref[...], w_ref[...])).astype(BF16)


def _in_b_call(h, w_all, layer):
    rows = h.shape[0]
    tm, tn = 1024, 1024
    return pl.pallas_call(
        _in_b_kernel,
        grid=(rows // tm, 7),
        in_specs=[
            pl.BlockSpec((tm, D_MODEL), lambda m, n: (m, 0)),
            pl.BlockSpec((None, tn, D_MODEL), lambda m, n: (layer, n + IN_A_TILES, 0)),
        ],
        out_specs=[
            pl.BlockSpec((tm, D_CONV), lambda m, n: (m, 0)),
            pl.BlockSpec((tm, D_CONV), lambda m, n: (m, 0)),
            pl.BlockSpec((tm, tn), lambda m, n: (m, jnp.clip(n - 3, 0, 3))),
        ],
        out_shape=[
            jax.ShapeDtypeStruct((rows, D_CONV), F32),
            jax.ShapeDtypeStruct((rows, D_CONV), BF16),
            jax.ShapeDtypeStruct((rows, 2 * D_MODEL), BF16),
        ],
        scratch_shapes=[pltpu.VMEM((tm, D_CONV), F32)],
        compiler_params=_params(("parallel", "arbitrary"), 48),
        name="in_proj_b",
    )(h, w_all)


def _tri(n, upper):
    r = lax.broadcasted_iota(jnp.int32, (n, n), 0)
    c = lax.broadcasted_iota(jnp.int32, (n, n), 1)
    keep = (r <= c) if upper else (c <= r)
    return jnp.where(keep, 1.0, 0.0).astype(BF16)


def _cum_p_kernel(lf_ref, fc_ref, fr_ref, *, seq):
    tril = _tri(CUM_CHUNK, upper=False)
    carry = jnp.zeros((1, LANES), F32)
    for i in range(seq // CUM_CHUNK):
        sl = slice(i * CUM_CHUNK, (i + 1) * CUM_CHUNK)
        hi, mid, lo = _split3(lf_ref[sl, :])
        y = _dot(tril, hi) + _dot(tril, mid) + _dot(tril, lo) + carry
        carry = y[CUM_CHUNK - 1:CUM_CHUNK, :]
        y2 = y * LOG2E
        fc_ref[sl, :] = y2
        fr_ref[:, sl] = y2.T[0:N_HEADS, :]


def _cum_p_call(lf, batch, seq):
    kern = functools.partial(_cum_p_kernel, seq=seq)
    return pl.pallas_call(
        kern,
        grid=(batch,),
        in_specs=[pl.BlockSpec((seq, LANES), lambda b: (b, 0))],
        out_specs=[
            pl.BlockSpec((seq, LANES), lambda b: (b, 0)),
            pl.BlockSpec((None, N_HEADS, seq), lambda b: (b, 0, 0)),
        ],
        out_shape=[
            jax.ShapeDtypeStruct((batch * seq, LANES), F32),
            jax.ShapeDtypeStruct((batch, N_HEADS, seq), F32),
        ],
        compiler_params=_params(("parallel",), 32),
        name="cum_prompt",
    )(lf)


def _cum_s_kernel(clf_ref, lfn_ref, g_ref, fnc_ref, fnr_ref, f_sc, *, past, new):
    triu = _tri(CUM_CHUNK, upper=True)
    carry = jnp.zeros((N_HEADS, 1), F32)
    for i in range(past // CUM_CHUNK):
        sl = slice(i * CUM_CHUNK, (i + 1) * CUM_CHUNK)
        hi, mid, lo = _split3(clf_ref[:, sl])
        y = _dot(hi, triu) + _dot(mid, triu) + _dot(lo, triu) + carry
        carry = y[:, CUM_CHUNK - 1:CUM_CHUNK]
        f_sc[:, sl] = y
    g_ref[...] = (carry - f_sc[...]) * LOG2E
    tril = _tri(LANES, upper=False)
    xn = jnp.concatenate([lfn_ref[...], jnp.zeros((LANES - new, LANES), F32)], axis=0)
    hi, mid, lo = _split3(xn)
    yn = (_dot(tril, hi) + _dot(tril, mid) + _dot(tril, lo)) * LOG2E
    fnc_ref[...] = yn[0:new, :]
    fnr_ref[...] = yn.T[0:N_HEADS, :]


def _cum_s_call(clf_t, lf_new, layer, batch, past, new):
    kern = functools.partial(_cum_s_kernel, past=past, new=new)
    return pl.pallas_call(
        kern,
        grid=(batch,),
        in_specs=[
            pl.BlockSpec((None, None, N_HEADS, past), lambda b: (layer, b, 0, 0)),
            pl.BlockSpec((new, LANES), lambda b: (b, 0)),
        ],
        out_specs=[
            pl.BlockSpec((None, N_HEADS, past), lambda b: (b, 0, 0)),
            pl.BlockSpec((new, LANES), lambda b: (b, 0)),
            pl.BlockSpec((None, N_HEADS, LANES), lambda b: (b, 0, 0)),
        ],
        out_shape=[
            jax.ShapeDtypeStruct((batch, N_HEADS, past), F32),
            jax.ShapeDtypeStruct((batch * new, LANES), F32),
            jax.ShapeDtypeStruct((batch, N_HEADS, LANES), F32),
        ],
        scratch_shapes=[pltpu.VMEM((N_HEADS, past), F32)],
        compiler_params=_params(("parallel",), 32),
        name="cum_sample",
    )(clf_t, lf_new)


def _softmax_parts(carry, t, fq):
    m, l, acc = carry
    m_new = jnp.maximum(m, jnp.max(t, axis=-1, keepdims=True) + fq)
    alpha = jnp.exp2(m - m_new)
    p = jnp.exp2(t + (fq - m_new))
    return m_new, alpha * l + jnp.sum(p, axis=-1, keepdims=True), alpha * acc, p.astype(BF16)


def _softmax_step(carry, t, fq, v):
    m_new, l, acc, p = _softmax_parts(carry, t, fq)
    return m_new, l, acc + _dot(p, v)


def _causal(s):
    row = lax.broadcasted_iota(jnp.int32, s.shape, 0)
    col = lax.broadcasted_iota(jnp.int32, s.shape, 1)
    return jnp.where(col <= row, s, NEG)


def _attn_p_kernel(q_ref, k_ref, v_ref, fc_ref, fr_ref, ga_ref, o_ref, *, seq, tq):
    hd = pl.program_id(1)
    nq = seq // tq
    lane = lax.broadcasted_iota(jnp.int32, (seq, LANES), 1)
    fq_all = jnp.sum(jnp.where(lane == hd, fc_ref[...], 0.0), axis=-1, keepdims=True)
    fr = fr_ref[pl.ds(hd, 1), :]
    blk = lambda i: slice(i * tq, (i + 1) * tq)

    def scores(i, j):
        t = _nt_dot(q_ref[blk(i), :], k_ref[blk(j), :]) - fr[:, blk(j)]
        return _causal(t) if i == j else t

    pairs = [(i, j) for i in range(nq) for j in range(i + 1)]
    t_next = scores(*pairs[0])
    carry = None
    for n, (i, j) in enumerate(pairs):
        t = t_next
        if n + 1 < len(pairs):
            t_next = scores(*pairs[n + 1])
        if j == 0:
            carry = (jnp.full((tq, 1), NEG, F32), jnp.zeros((tq, 1), F32), jnp.zeros((tq, HEAD_DIM), F32))
        carry = _softmax_step(carry, t, fq_all[blk(i), :], v_ref[blk(j), :])
        if j == i:
            _, l, acc = carry
            o_ref[blk(i), :] = (acc / l * ga_ref[blk(i), :].astype(F32)).astype(BF16)


def _attn_p_call(q, k, v, f_col, f_row, ga, batch, seq):
    tq = 512
    head = pl.BlockSpec((seq, HEAD_DIM), lambda b, h: (b, h))
    return pl.pallas_call(
        functools.partial(_attn_p_kernel, seq=seq, tq=tq),
        grid=(batch, N_HEADS),
        in_specs=[
            head, head, head,
            pl.BlockSpec((seq, LANES), lambda b, h: (b, 0)),
            pl.BlockSpec((None, N_HEADS, seq), lambda b, h: (b, 0, 0)),
            head,
        ],
        out_specs=head,
        out_shape=jax.ShapeDtypeStruct((batch * seq, D_ATTN), BF16),
        compiler_params=_params(("parallel", "parallel"), 32),
        name="attn_prompt",
    )(q, k, v, f_col, f_row, ga)


def _attn_s_kernel(q_ref, ck_ref, cv_ref, g_ref, fnc_ref, fnr_ref, kn_ref, vn_ref, ga_ref,
                   o_ref, m_sc, l_sc, acc_sc, *, new, tk):
    j = pl.program_id(1)
    last = pl.num_programs(1) - 1

    @pl.when(j == 0)
    def _():
        m_sc[...] = jnp.full(m_sc.shape, NEG, F32)
        l_sc[...] = jnp.zeros(l_sc.shape, F32)
        acc_sc[...] = jnp.zeros(acc_sc.shape, F32)

    heads = range(N_HEADS)
    sls = [slice(hd * HEAD_DIM, (hd + 1) * HEAD_DIM) for hd in heads]
    state = [(m_sc[hd][:, 0:1], l_sc[hd][:, 0:1], acc_sc[:, sls[hd]]) for hd in heads]
    fq = [fnc_ref[:, hd:hd + 1] for hd in heads]

    ts = [_nt_dot(q_ref[:, sls[hd]], ck_ref[pl.ds(hd, tk, stride=N_HEADS), :].astype(BF16)) + g_ref[hd:hd + 1, :]
          for hd in heads]
    parts = [_softmax_parts(state[hd], ts[hd], fq[hd]) for hd in heads]
    out = []
    for hd in heads:
        m, l, acc, p = parts[hd]
        out.append((m, l, acc + _dot(p, cv_ref[pl.ds(hd, tk, stride=N_HEADS), :].astype(BF16))))
    for hd in heads:
        m, l, acc = out[hd]
        m_sc[hd] = jnp.broadcast_to(m, (new, LANES))
        l_sc[hd] = jnp.broadcast_to(l, (new, LANES))
        acc_sc[:, sls[hd]] = acc

    @pl.when(j == last)
    def _():
        pad = jnp.zeros((LANES - new, HEAD_DIM), BF16)
        ts = [_causal(_nt_dot(q_ref[:, sls[hd]], jnp.concatenate([kn_ref[:, sls[hd]], pad], axis=0))
                      - fnr_ref[hd:hd + 1, :]) for hd in heads]
        parts = [_softmax_parts(out[hd], ts[hd], fq[hd]) for hd in heads]
        for hd in heads:
            _, l, acc, p = parts[hd]
            acc = acc + _dot(p, jnp.concatenate([vn_ref[:, sls[hd]], pad], axis=0))
            o_ref[:, sls[hd]] = (acc / l * ga_ref[:, sls[hd]].astype(F32)).astype(BF16)


def _attn_s_call(q, cache_k, cache_v, g_row, fn_col, fn_row, k_new, v_new, ga, layer, batch, past, new):
    tk = 2048
    row = lambda width: pl.BlockSpec((new, width), lambda b, j: (b, 0))
    cache = pl.BlockSpec((None, None, tk * N_HEADS, HEAD_DIM), lambda b, j: (layer, b, j, 0))
    return pl.pallas_call(
        functools.partial(_attn_s_kernel, new=new, tk=tk),
        grid=(batch, past // tk),
        in_specs=[
            row(D_ATTN), cache, cache,
            pl.BlockSpec((None, N_HEADS, tk), lambda b, j: (b, 0, j)),
            row(LANES),
            pl.BlockSpec((None, N_HEADS, LANES), lambda b, j: (b, 0, 0)),
            row(D_ATTN), row(D_ATTN), row(D_ATTN),
        ],
        out_specs=row(D_ATTN),
        out_shape=jax.ShapeDtypeStruct((batch * new, D_ATTN), BF16),
        scratch_shapes=[
            pltpu.VMEM((N_HEADS, new, LANES), F32),
            pltpu.VMEM((N_HEADS, new, LANES), F32),
            pltpu.VMEM((new, D_ATTN), F32),
        ],
        compiler_params=_params(("parallel", "arbitrary"), 52),
        name="attn_sample",
    )(q, cache_k, cache_v, g_row, fn_col, fn_row, k_new, v_new, ga)


CONV_ROWS = 64


def _conv_kernel(u_ref, halo_ref, gb_ref, w_ref, cb_ref, lg_ref, lb_ref, o_ref, ext_sc, cv_sc,
                 *, tt, tiles_per_batch, halo_from_u):
    nc = D_CONV // LANES
    chunk = lambda c: slice(c * LANES, (c + 1) * LANES)
    for c in range(nc):
        ext_sc[c, HALO_PAD:HALO_PAD + tt, :] = u_ref[:, chunk(c)]
    if halo_from_u:
        first = lax.rem(pl.program_id(0), tiles_per_batch) == 0

        @pl.when(first)
        def _():
            ext_sc[:, 0:HALO_PAD, :] = jnp.zeros((nc, HALO_PAD, LANES), F32)

        @pl.when(jnp.logical_not(first))
        def _():
            for c in range(nc):
                ext_sc[c, 0:HALO_PAD, :] = halo_ref[:, chunk(c)]
    else:
        for c in range(nc):
            ext_sc[c, HALO_PAD - HALO:HALO_PAD, :] = halo_ref[:, chunk(c)]

    base = HALO_PAD - HALO
    for c in range(nc):
        w_taps = w_ref[:, chunk(c)]
        bias = cb_ref[:, chunk(c)]

        def rows_body(r, carry, c=c, w_taps=w_taps, bias=bias):
            r0 = pl.multiple_of(r * CONV_ROWS, CONV_ROWS)
            acc = jnp.zeros((CONV_ROWS, LANES), F32)
            for j in range(CONV_WIDTH):
                acc = acc + w_taps[j:j + 1, :] * ext_sc[c, pl.ds(r0 + (base + j), CONV_ROWS, stride=1), :]
            cv_sc[pl.ds(r0, CONV_ROWS), chunk(c)] = acc + bias
            return carry

        lax.fori_loop(0, tt // CONV_ROWS, rows_body, 0)

    cv = cv_sc[...]
    mu = jnp.mean(cv, axis=-1, keepdims=True)
    d = cv - mu
    var = jnp.mean(d * d, axis=-1, keepdims=True)
    y = d * lax.rsqrt(var + EPS) * lg_ref[...] + lb_ref[...]
    o_ref[...] = (_silu(y) * gb_ref[...].astype(F32)).astype(BF16)


def _conv_call(u, halo, gb, conv_w, conv_b, ln_g, ln_b, layer, tt, tiles_per_batch, halo_from_u):
    rows = u.shape[0]
    tile = pl.BlockSpec((tt, D_CONV), lambda t: (t, 0))
    if halo_from_u:
        per = tt // HALO_PAD
        halo_spec = pl.BlockSpec((HALO_PAD, D_CONV), lambda t: (jnp.maximum(t * per - 1, 0), 0))
    else:
        halo_spec = pl.BlockSpec((None, None, HALO, D_CONV), lambda t: (layer, t, 0, 0))
    kern = functools.partial(_conv_kernel, tt=tt, tiles_per_batch=tiles_per_batch, halo_from_u=halo_from_u)
    return pl.pallas_call(
        kern,
        grid=(rows // tt,),
        in_specs=[
            tile, halo_spec, tile,
            _layer_spec(conv_w.shape, layer), _layer_spec(conv_b.shape, layer),
            _layer_spec(ln_g.shape, layer), _layer_spec(ln_b.shape, layer),
        ],
        out_specs=tile,
        out_shape=jax.ShapeDtypeStruct((rows, D_CONV), BF16),
        scratch_shapes=[
            pltpu.VMEM((D_CONV // LANES, HALO_PAD + tt, LANES), F32),
            pltpu.VMEM((tt, D_CONV), F32),
        ],
        compiler_params=_params(("parallel",), 32),
        name="conv_module",
    )(u, halo, gb, conv_w, conv_b, ln_g, ln_b)


OUT_COLS = 512


def _out_kernel(ya_ref, yb_ref, mab_ref, x_ref, mod_ref, woa_ref, wob_ref, wout_ref, o_ref, mg_sc,
                *, tm, rows_per_batch, batch0):
    m = pl.program_id(0)
    ya = ya_ref[...]
    yb = yb_ref[...]
    for c in range(D_MODEL // OUT_COLS):
        cs = slice(c * OUT_COLS, (c + 1) * OUT_COLS)
        cs_b = slice(D_MODEL + c * OUT_COLS, D_MODEL + (c + 1) * OUT_COLS)
        pa = _dot(ya, woa_ref[:, cs])
        pb = _dot(yb, wob_ref[:, cs])
        merged = mab_ref[:, cs].astype(F32) * pa + mab_ref[:, cs_b].astype(F32) * pb
        mg_sc[:, cs] = merged.astype(BF16)
    out = _dot(mg_sc[...], wout_ref[...])
    sub = min(tm, rows_per_batch)
    for i in range(tm // sub):
        b = _batch_of(m * tm + i * sub, rows_per_batch, batch0)
        gate = mod_ref[pl.ds(b, 1), 2 * D_MODEL:3 * D_MODEL]
        rs = slice(i * sub, (i + 1) * sub)
        o_ref[rs, :] = x_ref[rs, :] + gate * out[rs, :]


def _out_call(ya, yb, mab, x, mod, w_oa, w_ob, w_out, layer, rows_per_batch, batch0):
    rows = x.shape[0]
    tm = 512
    row_blk = lambda width: pl.BlockSpec((tm, width), lambda m: (m, 0))
    kern = functools.partial(_out_kernel, tm=tm, rows_per_batch=rows_per_batch, batch0=batch0)
    return pl.pallas_call(
        kern,
        grid=(rows // tm,),
        in_specs=[
            row_blk(D_ATTN), row_blk(D_CONV), row_blk(2 * D_MODEL), row_blk(D_MODEL),
            _layer_spec(mod.shape, layer), _layer_spec(w_oa.shape, layer),
            _layer_spec(w_ob.shape, layer), _layer_spec(w_out.shape, layer),
        ],
        out_specs=row_blk(D_MODEL),
        out_shape=jax.ShapeDtypeStruct((rows, D_MODEL), F32),
        scratch_shapes=[pltpu.VMEM((tm, D_MODEL), BF16)],
        compiler_params=_params(("parallel",), 56),
        name="out_proj",
    )(ya, yb, mab, x, mod, w_oa, w_ob, w_out)


PREP_TILE = 1024
FORGET_COL = 3 * D_ATTN


def _prep_in_kernel(a_ref, b_ref, o_ref):
    i = pl.program_id(1)

    @pl.when(i < FORGET_COL // PREP_TILE)
    def _():
        o_ref[...] = a_ref[...].astype(BF16)

    @pl.when(i >= FORGET_COL // PREP_TILE)
    def _():
        o_ref[0:PREP_TILE - N_HEADS, :] = a_ref[N_HEADS:PREP_TILE, :].astype(BF16)
        o_ref[PREP_TILE - N_HEADS:PREP_TILE, :] = b_ref[...].astype(BF16)


def _prep_in_call(w_t):
    depth, d_in, d = w_t.shape
    n_tiles = (d_in - N_HEADS) // PREP_TILE
    per = PREP_TILE // N_HEADS
    return pl.pallas_call(
        _prep_in_kernel,
        grid=(depth, n_tiles),
        in_specs=[
            pl.BlockSpec((None, PREP_TILE, d), lambda l, i: (l, i, 0)),
            pl.BlockSpec((None, N_HEADS, d), lambda l, i: (l, (i + 1) * per, 0)),
        ],
        out_specs=pl.BlockSpec((None, PREP_TILE, d), lambda l, i: (l, i, 0)),
        out_shape=jax.ShapeDtypeStruct((depth, n_tiles * PREP_TILE, d), BF16),
        compiler_params=_params(("parallel", "parallel"), 40),
        name="prep_w_in",
    )(w_t, w_t)


def _cast_kernel(x_ref, o_ref):
    o_ref[...] = x_ref[...].astype(BF16)


def _cast_call(w):
    depth, k, n = w.shape
    rows = 512
    blk = pl.BlockSpec((None, rows, n), lambda l, r: (l, r, 0))
    return pl.pallas_call(
        _cast_kernel,
        grid=(depth, k // rows),
        in_specs=[blk],
        out_specs=blk,
        out_shape=jax.ShapeDtypeStruct(w.shape, BF16),
        compiler_params=_params(("parallel", "parallel"), 32),
        name="cast_bf16",
    )(w)


def _layer_group(x, mod, rows_per_batch, batch0, wl, layer, depth, stacked_prev):
    h = _norm_call(x, mod, wl["norm_g"], layer, rows_per_batch, batch0)
    lf, lf_all, q, k_all, v_all, kb, vb, ga = _in_a_call(
        h, wl["w_all"], wl["w_f"], wl["b_f"], wl["q_g"], wl["k_g"], layer, depth, stacked_prev)
    u, gb, mab = _in_b_call(h, wl["w_all"], layer)
    return lf, q, (lf_all, k_all, v_all), kb, vb, ga, u, gb, mab


def kernel(x_prompt, x_sample, c_prompt, c_sample, cache_k, cache_v, cache_logf, state_conv, w_ada, b_ada, norm_g, w_in, b_f, q_norm_g, k_norm_g, conv_w, conv_b, conv_ln_g, conv_ln_b, w_oa, w_ob, w_out):
    batch, seq, d = x_prompt.shape
    dec_batch, dec_seq, _ = x_sample.shape
    depth = w_ada.shape[0]
    past = cache_k.shape[2]
    assert d == D_MODEL and seq % 1024 == 0 and (dec_batch * dec_seq) % 1024 == 0
    assert dec_seq >= HALO and seq >= HALO and dec_seq <= LANES

    nb = batch + dec_batch
    nb_pad = -(-nb // 8) * 8
    c_all = jnp.concatenate([c_prompt, c_sample, jnp.zeros((nb_pad - nb, d), F32)], axis=0)
    mod_all = _ada_call(c_all, w_ada, b_ada)

    xp = x_prompt.reshape(batch * seq, d)
    xs = x_sample.reshape(dec_batch * dec_seq, d)
    cache_k2 = cache_k.reshape(depth, dec_batch, past * N_HEADS, HEAD_DIM)
    cache_v2 = cache_v.reshape(depth, dec_batch, past * N_HEADS, HEAD_DIM)
    cache_lf_t = jnp.swapaxes(cache_logf, 2, 3)

    w_t = jnp.swapaxes(w_in, 1, 2)
    w_all = _prep_in_call(w_t)
    w_f = jnp.pad(w_t[:, FORGET_COL:FORGET_COL + N_HEADS, :], ((0, 0), (0, LANES - N_HEADS), (0, 0))).astype(BF16)
    woa, wob, wout = _cast_call(w_oa), _cast_call(w_ob), _cast_call(w_out)
    wl = dict(
        norm_g=norm_g.reshape(depth, 1, d), w_all=w_all, w_f=w_f,
        b_f=jnp.pad(b_f, ((0, 0), (0, LANES - N_HEADS))).reshape(depth, 1, LANES),
        q_g=q_norm_g.reshape(depth, 1, HEAD_DIM), k_g=k_norm_g.reshape(depth, 1, HEAD_DIM),
    )
    cb = conv_b.reshape(depth, 1, D_CONV)
    lg, lb = conv_ln_g.reshape(depth, 1, D_CONV), conv_ln_b.reshape(depth, 1, D_CONV)

    conv_p, conv_s = [], []
    st_p = st_s = None
    for l in range(depth):
        lf, q, st_p, kb, vb, ga, u, gb, mab = _layer_group(xp, mod_all, seq, 0, wl, l, depth, st_p)
        f_col, f_row = _cum_p_call(lf, batch, seq)
        ya = _attn_p_call(q, kb, vb, f_col, f_row, ga, batch, seq)
        yb = _conv_call(u, u, gb, conv_w, cb, lg, lb, l, 512, seq // 512, True)
        xp = _out_call(ya, yb, mab, xp, mod_all, woa, wob, wout, l, seq, 0)
        conv_p.append(u.reshape(batch, seq, D_CONV)[:, seq - HALO:, :])

        lf, q, st_s, kb, vb, ga, u, gb, mab = _layer_group(xs, mod_all, dec_seq, batch, wl, l, depth, st_s)
        g_row, fn_col, fn_row = _cum_s_call(cache_lf_t, lf, l, dec_batch, past, dec_seq)
        ya = _attn_s_call(q, cache_k2, cache_v2, g_row, fn_col, fn_row, kb, vb, ga,
                          l, dec_batch, past, dec_seq)
        yb = _conv_call(u, state_conv, gb, conv_w, cb, lg, lb, l, dec_seq, 1, False)
        xs = _out_call(ya, yb, mab, xs, mod_all, woa, wob, wout, l, dec_seq, batch)
        conv_s.append(u.reshape(dec_batch, dec_seq, D_CONV)[:, dec_seq - HALO:, :])

    cache_p = (depth, batch, seq, N_HEADS, HEAD_DIM)
    cache_s = (depth, dec_batch, dec_seq, N_HEADS, HEAD_DIM)
    return (xp.reshape(batch, seq, d), xs.reshape(dec_batch, dec_seq, d),
            st_p[1].reshape(cache_p), st_p[2].reshape(cache_p),
            st_p[0].reshape(depth, batch, seq, N_HEADS), jnp.stack(conv_p),
            st_s[1].reshape(cache_s), st_s[2].reshape(cache_s),
            st_s[0].reshape(depth, dec_batch, dec_seq, N_HEADS), jnp.stack(conv_s))
```

```python
import functools

import jax
import jax.numpy as jnp
from jax import lax
from jax.experimental import pallas as pl
from jax.experimental.pallas import tpu as pltpu

D_MODEL = 2048
N_HEADS = 8
HEAD_DIM = 128
D_ATTN = N_HEADS * HEAD_DIM
D_CONV = 1024
CONV_WIDTH = 31
HALO = CONV_WIDTH - 1
EPS = 1e-6
NEG = -1e30
LOG2E = 1.4426950408889634

LANES = 128
HALO_PAD = 32
CUM_CHUNK = 256
MIB = 1024 * 1024

BF16 = jnp.bfloat16
F32 = jnp.float32


def _params(sem, vmem_mib):
    return pltpu.CompilerParams(dimension_semantics=sem, vmem_limit_bytes=vmem_mib * MIB)


def _layer_spec(shape, layer):
    nd = len(shape) - 1
    return pl.BlockSpec((None,) + tuple(shape[1:]), lambda *_: (layer,) + (0,) * nd, pipeline_mode=pl.Buffered(1))


def _sigmoid(x):
    return 1.0 / (1.0 + jnp.exp(-x))


def _silu(x):
    return x * _sigmoid(x)


def _log_sigmoid(x):
    return jnp.minimum(x, 0.0) - jnp.log1p(jnp.exp(-jnp.abs(x)))


def _nt_dot(a, b):
    return lax.dot_general(a, b, (((1,), (1,)), ((), ())), preferred_element_type=F32)


def _dot(a, b):
    return jnp.dot(a, b, preferred_element_type=F32)


def _split3(x):
    hi = x.astype(BF16)
    r1 = x - hi.astype(F32)
    mid = r1.astype(BF16)
    lo = (r1 - mid.astype(F32)).astype(BF16)
    return hi, mid, lo


def _ada_kernel(c_ref, w_ref, b_ref, o_ref):
    s = _silu(c_ref[...]).astype(BF16)
    o_ref[...] = _dot(s, w_ref[...].astype(BF16)) + b_ref[...]


def _ada_call(c_all, w_ada, b_ada):
    depth, d, e = w_ada.shape
    nb = c_all.shape[0]
    tn = 1024
    return pl.pallas_call(
        _ada_kernel,
        grid=(depth, e // tn),
        in_specs=[
            pl.BlockSpec((nb, d), lambda l, n: (0, 0)),
            pl.BlockSpec((None, d, tn), lambda l, n: (l, 0, n)),
            pl.BlockSpec((None, 1, tn), lambda l, n: (l, 0, n)),
        ],
        out_specs=pl.BlockSpec((None, nb, tn), lambda l, n: (l, 0, n)),
        out_shape=jax.ShapeDtypeStruct((depth, nb, e), F32),
        compiler_params=_params(("parallel", "parallel"), 32),
        name="ada_mod",
    )(c_all, w_ada, b_ada.reshape(depth, 1, e))


def _batch_of(row, rows_per_batch, batch0):
    return batch0 + lax.div(row, rows_per_batch)


def _mod_norm(x, g, scale, shift):
    ms = jnp.mean(x * x, axis=-1, keepdims=True)
    y = x * lax.rsqrt(ms + EPS) * g
    return (y * (1.0 + scale) + shift).astype(BF16)


def _norm_kernel(x_ref, mod_ref, g_ref, h_ref, *, tm, rows_per_batch, batch0):
    m = pl.program_id(0)
    sub = min(tm, rows_per_batch)
    for i in range(tm // sub):
        b = _batch_of(m * tm + i * sub, rows_per_batch, batch0)
        shift = mod_ref[pl.ds(b, 1), 0:D_MODEL]
        scale = mod_ref[pl.ds(b, 1), D_MODEL:2 * D_MODEL]
        rs = slice(i * sub, (i + 1) * sub)
        h_ref[rs, :] = _mod_norm(x_ref[rs, :], g_ref[...], scale, shift)


def _norm_call(x, mod, g, layer, rows_per_batch, batch0):
    rows = x.shape[0]
    tm = 1024
    kern = functools.partial(_norm_kernel, tm=tm, rows_per_batch=rows_per_batch, batch0=batch0)
    return pl.pallas_call(
        kern,
        grid=(rows // tm,),
        in_specs=[
            pl.BlockSpec((tm, D_MODEL), lambda m: (m, 0)),
            _layer_spec(mod.shape, layer),
            _layer_spec(g.shape, layer),
        ],
        out_specs=pl.BlockSpec((tm, D_MODEL), lambda m: (m, 0)),
        out_shape=jax.ShapeDtypeStruct((rows, D_MODEL), BF16),
        compiler_params=_params(("parallel",), 32),
        name="norm_mod",
    )(x, mod, g)


def _head_rmsnorm(a, g):
    ms = jnp.mean(a * a, axis=-1, keepdims=True)
    return a * lax.rsqrt(ms + EPS) * g


def _store_heads(y, hd, tm, cache_ref, bf_ref):
    cache_ref[pl.ds(hd, tm, stride=N_HEADS), :] = y
    bf_ref[:, hd * HEAD_DIM:(hd + 1) * HEAD_DIM] = y.astype(BF16)


def _in_a_kernel(*refs, tm, n_alias):
    (h_ref, w_ref, wf_ref, bf_ref, qg_ref, kg_ref) = refs[:6]
    (lf_ref, lf8_ref, q_ref, k_ref, v_ref, kb_ref, vb_ref, ga_ref) = refs[6 + n_alias:]
    n = pl.program_id(1)

    def head(acc, hd):
        return acc[:, hd * HEAD_DIM:(hd + 1) * HEAD_DIM]

    @pl.when(n == 0)
    def _():
        f = _nt_dot(h_ref[...], wf_ref[...]) + bf_ref[...]
        lf = _log_sigmoid(f)
        lf_ref[...] = lf
        lf8_ref[...] = lf[:, 0:N_HEADS]
        acc = _nt_dot(h_ref[...], w_ref[...])
        for hd in range(N_HEADS):
            y = _head_rmsnorm(head(acc, hd), qg_ref[...]) * (HEAD_DIM ** -0.5 * LOG2E)
            q_ref[:, hd * HEAD_DIM:(hd + 1) * HEAD_DIM] = y.astype(BF16)

    @pl.when(n == 1)
    def _():
        acc = _nt_dot(h_ref[...], w_ref[...])
        for hd in range(N_HEADS):
            _store_heads(_head_rmsnorm(head(acc, hd), kg_ref[...]), hd, tm, k_ref, kb_ref)

    @pl.when(n == 2)
    def _():
        acc = _nt_dot(h_ref[...], w_ref[...])
        for hd in range(N_HEADS):
            _store_heads(head(acc, hd), hd, tm, v_ref, vb_ref)

    @pl.when(n == 3)
    def _():
        ga_ref[...] = _silu(_nt_dot(h_ref[...], w_ref[...])).astype(BF16)


def _in_a_call(h, w_all, w_f, b_f, q_g, k_g, layer, depth, stacked_prev):
    rows = h.shape[0]
    tm, tn = 1024, 1024
    row_blk = lambda width: pl.BlockSpec((tm, width), lambda m, n: (m, 0))
    cache_blk = pl.BlockSpec((None, tm * N_HEADS, HEAD_DIM), lambda m, n: (layer, m, 0))
    cache_shape = jax.ShapeDtypeStruct((depth, rows * N_HEADS, HEAD_DIM), F32)
    alias_in = [] if stacked_prev is None else list(stacked_prev)
    n_in = 6
    return pl.pallas_call(
        functools.partial(_in_a_kernel, tm=tm, n_alias=len(alias_in)),
        grid=(rows // tm, 4),
        in_specs=[
            row_blk(D_MODEL),
            pl.BlockSpec((None, tn, D_MODEL), lambda m, n: (layer, n, 0)),
            _layer_spec(w_f.shape, layer),
            _layer_spec(b_f.shape, layer),
            _layer_spec(q_g.shape, layer),
            _layer_spec(k_g.shape, layer),
        ] + [pl.BlockSpec(memory_space=pl.ANY)] * len(alias_in),
        out_specs=[row_blk(LANES),
                   pl.BlockSpec((None, tm, N_HEADS), lambda m, n: (layer, m, 0)),
                   row_blk(D_ATTN), cache_blk, cache_blk,
                   row_blk(D_ATTN), row_blk(D_ATTN), row_blk(D_ATTN)],
        out_shape=[
            jax.ShapeDtypeStruct((rows, LANES), F32),
            jax.ShapeDtypeStruct((depth, rows, N_HEADS), F32),
            jax.ShapeDtypeStruct((rows, D_ATTN), BF16),
            cache_shape,
            cache_shape,
            jax.ShapeDtypeStruct((rows, D_ATTN), BF16),
            jax.ShapeDtypeStruct((rows, D_ATTN), BF16),
            jax.ShapeDtypeStruct((rows, D_ATTN), BF16),
        ],
        input_output_aliases={n_in + i: out for i, out in zip(range(len(alias_in)), (1, 3, 4))},
        compiler_params=_params(("parallel", "arbitrary"), 58),
        name="in_proj_a",
    )(h, w_all, w_f, b_f, q_g, k_g, *alias_in)


IN_A_TILES = 4


def _in_b_kernel(h_ref, w_ref, u_ref, gb_ref, mab_ref, lin_sc):
    n = pl.program_id(1)

    @pl.when(n == 0)
    def _():
        lin_sc[...] = _nt_dot(h_ref[...], w_ref[...])

    @pl.when(n == 1)
    def _():
        u_ref[...] = lin_sc[...] * _sigmoid(_nt_dot(h_ref[...], w_ref[...]))

    @pl.when(n == 2)
    def _():
        gb_ref[...] = _silu(_nt_dot(h_ref[...], w_ref[...])).astype(BF16)

    @pl.when(n > 2)
    def _():
        mab_ref[...] = _sigmoid(_nt_dot(h_ref[...], w_ref[...])).astype(BF16)


def _in_b_call(h, w_all, layer):
    rows = h.shape[0]
    tm, tn = 1024, 1024
    return pl.pallas_call(
        _in_b_kernel,
        grid=(rows // tm, 7),
        in_specs=[
            pl.BlockSpec((tm, D_MODEL), lambda m, n: (m, 0)),
            pl.BlockSpec((None, tn, D_MODEL), lambda m, n: (layer, n + IN_A_TILES, 0)),
        ],
        out_specs=[
            pl.BlockSpec((tm, D_CONV), lambda m, n: (m, 0)),
            pl.BlockSpec((tm, D_CONV), lambda m, n: (m, 0)),
            pl.BlockSpec((tm, tn), lambda m, n: (m, jnp.clip(n - 3, 0, 3))),
        ],
        out_shape=[
            jax.ShapeDtypeStruct((rows, D_CONV), F32),
            jax.ShapeDtypeStruct((rows, D_CONV), BF16),
            jax.ShapeDtypeStruct((rows, 2 * D_MODEL), BF16),
        ],
        scratch_shapes=[pltpu.VMEM((tm, D_CONV), F32)],
        compiler_params=_params(("parallel", "arbitrary"), 48),
        name="in_proj_b",
    )(h, w_all)


def _tri(n, upper):
    r = lax.broadcasted_iota(jnp.int32, (n, n), 0)
    c = lax.broadcasted_iota(jnp.int32, (n, n), 1)
    keep = (r <= c) if upper else (c <= r)
    return jnp.where(keep, 1.0, 0.0).astype(BF16)


def _cum_p_kernel(lf_ref, fc_ref, fr_ref, *, seq):
    tril = _tri(CUM_CHUNK, upper=False)
    carry = jnp.zeros((1, LANES), F32)
    for i in range(seq // CUM_CHUNK):
        sl = slice(i * CUM_CHUNK, (i + 1) * CUM_CHUNK)
        hi, mid, lo = _split3(lf_ref[sl, :])
        y = _dot(tril, hi) + _dot(tril, mid) + _dot(tril, lo) + carry
        carry = y[CUM_CHUNK - 1:CUM_CHUNK, :]
        y2 = y * LOG2E
        fc_ref[sl, :] = y2
        fr_ref[:, sl] = y2.T[0:N_HEADS, :]


def _cum_p_call(lf, batch, seq):
    kern = functools.partial(_cum_p_kernel, seq=seq)
    return pl.pallas_call(
        kern,
        grid=(batch,),
        in_specs=[pl.BlockSpec((seq, LANES), lambda b: (b, 0))],
        out_specs=[
            pl.BlockSpec((seq, LANES), lambda b: (b, 0)),
            pl.BlockSpec((None, N_HEADS, seq), lambda b: (b, 0, 0)),
        ],
        out_shape=[
            jax.ShapeDtypeStruct((batch * seq, LANES), F32),
            jax.ShapeDtypeStruct((batch, N_HEADS, seq), F32),
        ],
        compiler_params=_params(("parallel",), 32),
        name="cum_prompt",
    )(lf)


def _cum_s_kernel(clf_ref, lfn_ref, g_ref, fnc_ref, fnr_ref, f_sc, *, batch, past, new):
    triu = _tri(CUM_CHUNK, upper=True)
    carry = jnp.zeros((batch * N_HEADS, 1), F32)
    for i in range(past // CUM_CHUNK):
        sl = slice(i * CUM_CHUNK, (i + 1) * CUM_CHUNK)
        hi, mid, lo = _split3(clf_ref[:, sl])
        y = _dot(hi, triu) + _dot(mid, triu) + _dot(lo, triu) + carry
        carry = y[:, CUM_CHUNK - 1:CUM_CHUNK]
        f_sc[:, sl] = y
    g_ref[...] = (carry - f_sc[...]) * LOG2E
    tril = _tri(LANES, upper=False)
    pad = jnp.zeros((LANES - new, LANES), F32)
    for b in range(batch):
        rows = slice(b * new, (b + 1) * new)
        hi, mid, lo = _split3(jnp.concatenate([lfn_ref[rows, :], pad], axis=0))
        yn = (_dot(tril, hi) + _dot(tril, mid) + _dot(tril, lo)) * LOG2E
        fnc_ref[rows, :] = yn[0:new, :]
        fnr_ref[b] = yn.T[0:N_HEADS, :]


def _cum_s_call(clf_t, lf_new, layer, batch, past, new):
    kern = functools.partial(_cum_s_kernel, batch=batch, past=past, new=new)
    bh = batch * N_HEADS
    g_rows, fn_col, fn_row = pl.pallas_call(
        kern,
        grid=(1,),
        in_specs=[
            pl.BlockSpec((None, bh, past), lambda i: (layer, 0, 0)),
            pl.BlockSpec((batch * new, LANES), lambda i: (0, 0)),
        ],
        out_specs=[
            pl.BlockSpec((bh, past), lambda i: (0, 0)),
            pl.BlockSpec((batch * new, LANES), lambda i: (0, 0)),
            pl.BlockSpec((batch, N_HEADS, LANES), lambda i: (0, 0, 0)),
        ],
        out_shape=[
            jax.ShapeDtypeStruct((bh, past), F32),
            jax.ShapeDtypeStruct((batch * new, LANES), F32),
            jax.ShapeDtypeStruct((batch, N_HEADS, LANES), F32),
        ],
        scratch_shapes=[pltpu.VMEM((bh, past), F32)],
        compiler_params=_params(("arbitrary",), 32),
        name="cum_sample",
    )(clf_t, lf_new)
    return g_rows.reshape(batch, N_HEADS, past), fn_col, fn_row


def _softmax_parts(carry, t, fq):
    m, l, acc = carry
    m_new = jnp.maximum(m, jnp.max(t, axis=-1, keepdims=True) + fq)
    alpha = jnp.exp2(m - m_new)
    p = jnp.exp2(t + (fq - m_new))
    return m_new, alpha * l + jnp.sum(p, axis=-1, keepdims=True), alpha * acc, p.astype(BF16)


def _softmax_step(carry, t, fq, v):
    m_new, l, acc, p = _softmax_parts(carry, t, fq)
    return m_new, l, acc + _dot(p, v)


def _causal(s):
    row = lax.broadcasted_iota(jnp.int32, s.shape, 0)
    col = lax.broadcasted_iota(jnp.int32, s.shape, 1)
    return jnp.where(col <= row, s, NEG)


def _attn_p_kernel(q_ref, k_ref, v_ref, fc_ref, fr_ref, ga_ref, o_ref, *, seq, tq):
    hd = pl.program_id(1)
    nq = seq // tq
    lane = lax.broadcasted_iota(jnp.int32, (seq, LANES), 1)
    fq_all = jnp.sum(jnp.where(lane == hd, fc_ref[...], 0.0), axis=-1, keepdims=True)
    fr = fr_ref[pl.ds(hd, 1), :]
    blk = lambda i: slice(i * tq, (i + 1) * tq)

    def scores(i, j):
        t = _nt_dot(q_ref[blk(i), :], k_ref[blk(j), :]) - fr[:, blk(j)]
        return _causal(t) if i == j else t

    pairs = [(i, j) for i in range(nq) for j in range(i + 1)]
    t_next = scores(*pairs[0])
    carry = None
    for n, (i, j) in enumerate(pairs):
        t = t_next
        if n + 1 < len(pairs):
            t_next = scores(*pairs[n + 1])
        if j == 0:
            carry = (jnp.full((tq, 1), NEG, F32), jnp.zeros((tq, 1), F32), jnp.zeros((tq, HEAD_DIM), F32))
        carry = _softmax_step(carry, t, fq_all[blk(i), :], v_ref[blk(j), :])
        if j == i:
            _, l, acc = carry
            o_ref[blk(i), :] = (acc / l * ga_ref[blk(i), :].astype(F32)).astype(BF16)


def _attn_p_call(q, k, v, f_col, f_row, ga, batch, seq):
    tq = 512
    head = pl.BlockSpec((seq, HEAD_DIM), lambda b, h: (b, h))
    return pl.pallas_call(
        functools.partial(_attn_p_kernel, seq=seq, tq=tq),
        grid=(batch, N_HEADS),
        in_specs=[
            head, head, head,
            pl.BlockSpec((seq, LANES), lambda b, h: (b, 0)),
            pl.BlockSpec((None, N_HEADS, seq), lambda b, h: (b, 0, 0)),
            head,
        ],
        out_specs=head,
        out_shape=jax.ShapeDtypeStruct((batch * seq, D_ATTN), BF16),
        compiler_params=_params(("parallel", "parallel"), 32),
        name="attn_prompt",
    )(q, k, v, f_col, f_row, ga)


def _attn_s_kernel(q_ref, ck_ref, cv_ref, g_ref, fnc_ref, fnr_ref, kn_ref, vn_ref, ga_ref,
                   o_ref, m_sc, l_sc, acc_sc, *, new, tk):
    j = pl.program_id(1)
    last = pl.num_programs(1) - 1

    @pl.when(j == 0)
    def _():
        m_sc[...] = jnp.full(m_sc.shape, NEG, F32)
        l_sc[...] = jnp.zeros(l_sc.shape, F32)
        acc_sc[...] = jnp.zeros(acc_sc.shape, F32)

    heads = range(N_HEADS)
    sls = [slice(hd * HEAD_DIM, (hd + 1) * HEAD_DIM) for hd in heads]
    state = [(m_sc[hd][:, 0:1], l_sc[hd][:, 0:1], acc_sc[:, sls[hd]]) for hd in heads]
    fq = [fnc_ref[:, hd:hd + 1] for hd in heads]

    ts = [_nt_dot(q_ref[:, sls[hd]], ck_ref[pl.ds(hd, tk, stride=N_HEADS), :].astype(BF16)) + g_ref[hd:hd + 1, :]
          for hd in heads]
    parts = [_softmax_parts(state[hd], ts[hd], fq[hd]) for hd in heads]
    out = []
    for hd in heads:
        m, l, acc, p = parts[hd]
        out.append((m, l, acc + _dot(p, cv_ref[pl.ds(hd, tk, stride=N_HEADS), :].astype(BF16))))
    for hd in heads:
        m, l, acc = out[hd]
        m_sc[hd] = jnp.broadcast_to(m, (new, LANES))
        l_sc[hd] = jnp.broadcast_to(l, (new, LANES))
        acc_sc[:, sls[hd]] = acc

    @pl.when(j == last)
    def _():
        pad = jnp.zeros((LANES - new, HEAD_DIM), BF16)
        ts = [_causal(_nt_dot(q_ref[:, sls[hd]], jnp.concatenate([kn_ref[:, sls[hd]], pad], axis=0))
                      - fnr_ref[hd:hd + 1, :]) for hd in heads]
        parts = [_softmax_parts(out[hd], ts[hd], fq[hd]) for hd in heads]
        for hd in heads:
            _, l, acc, p = parts[hd]
            acc = acc + _dot(p, jnp.concatenate([vn_ref[:, sls[hd]], pad], axis=0))
            o_ref[:, sls[hd]] = (acc / l * ga_ref[:, sls[hd]].astype(F32)).astype(BF16)


def _attn_s_call(q, cache_k, cache_v, g_row, fn_col, fn_row, k_new, v_new, ga, layer, batch, past, new):
    tk = 2048
    row = lambda width: pl.BlockSpec((new, width), lambda b, j: (b, 0))
    cache = pl.BlockSpec((None, None, tk * N_HEADS, HEAD_DIM), lambda b, j: (layer, b, j, 0))
    return pl.pallas_call(
        functools.partial(_attn_s_kernel, new=new, tk=tk),
        grid=(batch, past // tk),
        in_specs=[
            row(D_ATTN), cache, cache,
            pl.BlockSpec((None, N_HEADS, tk), lambda b, j: (b, 0, j)),
            row(LANES),
            pl.BlockSpec((None, N_HEADS, LANES), lambda b, j: (b, 0, 0)),
            row(D_ATTN), row(D_ATTN), row(D_ATTN),
        ],
        out_specs=row(D_ATTN),
        out_shape=jax.ShapeDtypeStruct((batch * new, D_ATTN), BF16),
        scratch_shapes=[
            pltpu.VMEM((N_HEADS, new, LANES), F32),
            pltpu.VMEM((N_HEADS, new, LANES), F32),
            pltpu.VMEM((new, D_ATTN), F32),
        ],
        compiler_params=_params(("parallel", "arbitrary"), 52),
        name="attn_sample",
    )(q, cache_k, cache_v, g_row, fn_col, fn_row, k_new, v_new, ga)


CONV_ROWS = 64


def _conv_kernel(u_ref, halo_ref, gb_ref, w_ref, cb_ref, lg_ref, lb_ref, o_ref, ext_sc, cv_sc,
                 *, tt, tiles_per_batch, halo_from_u):
    nc = D_CONV // LANES
    chunk = lambda c: slice(c * LANES, (c + 1) * LANES)
    for c in range(nc):
        ext_sc[c, HALO_PAD:HALO_PAD + tt, :] = u_ref[:, chunk(c)]
    if halo_from_u:
        first = lax.rem(pl.program_id(0), tiles_per_batch) == 0

        @pl.when(first)
        def _():
            ext_sc[:, 0:HALO_PAD, :] = jnp.zeros((nc, HALO_PAD, LANES), F32)

        @pl.when(jnp.logical_not(first))
        def _():
            for c in range(nc):
                ext_sc[c, 0:HALO_PAD, :] = halo_ref[:, chunk(c)]
    else:
        for c in range(nc):
            ext_sc[c, HALO_PAD - HALO:HALO_PAD, :] = halo_ref[:, chunk(c)]

    base = HALO_PAD - HALO
    for c in range(nc):
        w_taps = w_ref[:, chunk(c)]
        bias = cb_ref[:, chunk(c)]

        def rows_body(r, carry, c=c, w_taps=w_taps, bias=bias):
            r0 = pl.multiple_of(r * CONV_ROWS, CONV_ROWS)
            acc = jnp.zeros((CONV_ROWS, LANES), F32)
            for j in range(CONV_WIDTH):
                acc = acc + w_taps[j:j + 1, :] * ext_sc[c, pl.ds(r0 + (base + j), CONV_ROWS, stride=1), :]
            cv_sc[pl.ds(r0, CONV_ROWS), chunk(c)] = acc + bias
            return carry

        lax.fori_loop(0, tt // CONV_ROWS, rows_body, 0)

    cv = cv_sc[...]
    mu = jnp.mean(cv, axis=-1, keepdims=True)
    d = cv - mu
    var = jnp.mean(d * d, axis=-1, keepdims=True)
    y = d * lax.rsqrt(var + EPS) * lg_ref[...] + lb_ref[...]
    o_ref[...] = (_silu(y) * gb_ref[...].astype(F32)).astype(BF16)


def _conv_call(u, halo, gb, conv_w, conv_b, ln_g, ln_b, layer, tt, tiles_per_batch, halo_from_u):
    rows = u.shape[0]
    tile = pl.BlockSpec((tt, D_CONV), lambda t: (t, 0))
    if halo_from_u:
        per = tt // HALO_PAD
        halo_spec = pl.BlockSpec((HALO_PAD, D_CONV), lambda t: (jnp.maximum(t * per - 1, 0), 0))
    else:
        halo_spec = pl.BlockSpec((None, None, HALO, D_CONV), lambda t: (layer, t, 0, 0))
    kern = functools.partial(_conv_kernel, tt=tt, tiles_per_batch=tiles_per_batch, halo_from_u=halo_from_u)
    return pl.pallas_call(
        kern,
        grid=(rows // tt,),
        in_specs=[
            tile, halo_spec, tile,
            _layer_spec(conv_w.shape, layer), _layer_spec(conv_b.shape, layer),
            _layer_spec(ln_g.shape, layer), _layer_spec(ln_b.shape, layer),
        ],
        out_specs=tile,
        out_shape=jax.ShapeDtypeStruct((rows, D_CONV), BF16),
        scratch_shapes=[
            pltpu.VMEM((D_CONV // LANES, HALO_PAD + tt, LANES), F32),
            pltpu.VMEM((tt, D_CONV), F32),
        ],
        compiler_params=_params(("parallel",), 32),
        name="conv_module",
    )(u, halo, gb, conv_w, conv_b, ln_g, ln_b)


OUT_COLS = 512


def _out_kernel(*refs, tm, rows_per_batch, batch0, next_norm):
    ya_ref, yb_ref, mab_ref, x_ref, mod_ref, woa_ref, wob_ref, wout_ref = refs[:8]
    if next_norm:
        modn_ref, gn_ref, o_ref, hn_ref, mg_sc = refs[8:]
    else:
        o_ref, mg_sc = refs[8:]
    m = pl.program_id(0)
    ya = ya_ref[...]
    yb = yb_ref[...]
    for c in range(D_MODEL // OUT_COLS):
        cs = slice(c * OUT_COLS, (c + 1) * OUT_COLS)
        cs_b = slice(D_MODEL + c * OUT_COLS, D_MODEL + (c + 1) * OUT_COLS)
        pa = _dot(ya, woa_ref[:, cs])
        pb = _dot(yb, wob_ref[:, cs])
        merged = mab_ref[:, cs].astype(F32) * pa + mab_ref[:, cs_b].astype(F32) * pb
        mg_sc[:, cs] = merged.astype(BF16)
    out = _dot(mg_sc[...], wout_ref[...])
    sub = min(tm, rows_per_batch)
    for i in range(tm // sub):
        b = _batch_of(m * tm + i * sub, rows_per_batch, batch0)
        gate = mod_ref[pl.ds(b, 1), 2 * D_MODEL:3 * D_MODEL]
        rs = slice(i * sub, (i + 1) * sub)
        x_new = x_ref[rs, :] + gate * out[rs, :]
        o_ref[rs, :] = x_new
        if next_norm:
            shift = modn_ref[pl.ds(b, 1), 0:D_MODEL]
            scale = modn_ref[pl.ds(b, 1), D_MODEL:2 * D_MODEL]
            hn_ref[rs, :] = _mod_norm(x_new, gn_ref[...], scale, shift)


def _out_call(ya, yb, mab, x, mod, norm_g, w_oa, w_ob, w_out, layer, next_norm, rows_per_batch, batch0):
    rows = x.shape[0]
    tm = 512
    row_blk = lambda width: pl.BlockSpec((tm, width), lambda m: (m, 0))
    kern = functools.partial(_out_kernel, tm=tm, rows_per_batch=rows_per_batch, batch0=batch0, next_norm=next_norm)
    in_specs = [
        row_blk(D_ATTN), row_blk(D_CONV), row_blk(2 * D_MODEL), row_blk(D_MODEL),
        _layer_spec(mod.shape, layer), _layer_spec(w_oa.shape, layer),
        _layer_spec(w_ob.shape, layer), _layer_spec(w_out.shape, layer),
    ]
    args = [ya, yb, mab, x, mod, w_oa, w_ob, w_out]
    out_specs = [row_blk(D_MODEL)]
    out_shape = [jax.ShapeDtypeStruct((rows, D_MODEL), F32)]
    if next_norm:
        in_specs += [_layer_spec(mod.shape, layer + 1), _layer_spec(norm_g.shape, layer + 1)]
        args += [mod, norm_g]
        out_specs.append(row_blk(D_MODEL))
        out_shape.append(jax.ShapeDtypeStruct((rows, D_MODEL), BF16))
    res = pl.pallas_call(
        kern,
        grid=(rows // tm,),
        in_specs=in_specs,
        out_specs=out_specs,
        out_shape=out_shape,
        scratch_shapes=[pltpu.VMEM((tm, D_MODEL), BF16)],
        compiler_params=_params(("parallel",), 58),
        name="out_proj",
    )(*args)
    return (res[0], res[1]) if next_norm else (res[0], None)


PREP_TILE = 1024
FORGET_COL = 3 * D_ATTN


def _prep_in_kernel(a_ref, b_ref, o_ref):
    i = pl.program_id(1)

    @pl.when(i < FORGET_COL // PREP_TILE)
    def _():
        o_ref[...] = a_ref[...].astype(BF16)

    @pl.when(i >= FORGET_COL // PREP_TILE)
    def _():
        o_ref[0:PREP_TILE - N_HEADS, :] = a_ref[N_HEADS:PREP_TILE, :].astype(BF16)
        o_ref[PREP_TILE - N_HEADS:PREP_TILE, :] = b_ref[...].astype(BF16)


def _prep_in_call(w_t):
    depth, d_in, d = w_t.shape
    n_tiles = (d_in - N_HEADS) // PREP_TILE
    per = PREP_TILE // N_HEADS
    return pl.pallas_call(
        _prep_in_kernel,
        grid=(depth, n_tiles),
        in_specs=[
            pl.BlockSpec((None, PREP_TILE, d), lambda l, i: (l, i, 0)),
            pl.BlockSpec((None, N_HEADS, d), lambda l, i: (l, (i + 1) * per, 0)),
        ],
        out_specs=pl.BlockSpec((None, PREP_TILE, d), lambda l, i: (l, i, 0)),
        out_shape=jax.ShapeDtypeStruct((depth, n_tiles * PREP_TILE, d), BF16),
        compiler_params=_params(("parallel", "parallel"), 40),
        name="prep_w_in",
    )(w_t, w_t)


def _cast_kernel(x_ref, o_ref):
    o_ref[...] = x_ref[...].astype(BF16)


def _cast_call(w):
    depth, k, n = w.shape
    rows = 512
    blk = pl.BlockSpec((None, rows, n), lambda l, r: (l, r, 0))
    return pl.pallas_call(
        _cast_kernel,
        grid=(depth, k // rows),
        in_specs=[blk],
        out_specs=blk,
        out_shape=jax.ShapeDtypeStruct(w.shape, BF16),
        compiler_params=_params(("parallel", "parallel"), 32),
        name="cast_bf16",
    )(w)


def _layer_group(x, h, mod, rows_per_batch, batch0, wl, layer, depth, stacked_prev):
    if h is None:
        h = _norm_call(x, mod, wl["norm_g"], layer, rows_per_batch, batch0)
    lf, lf_all, q, k_all, v_all, kb, vb, ga = _in_a_call(
        h, wl["w_all"], wl["w_f"], wl["b_f"], wl["q_g"], wl["k_g"], layer, depth, stacked_prev)
    u, gb, mab = _in_b_call(h, wl["w_all"], layer)
    return lf, q, (lf_all, k_all, v_all), kb, vb, ga, u, gb, mab


def kernel(x_prompt, x_sample, c_prompt, c_sample, cache_k, cache_v, cache_logf, state_conv, w_ada, b_ada, norm_g, w_in, b_f, q_norm_g, k_norm_g, conv_w, conv_b, conv_ln_g, conv_ln_b, w_oa, w_ob, w_out):
    batch, seq, d = x_prompt.shape
    dec_batch, dec_seq, _ = x_sample.shape
    depth = w_ada.shape[0]
    past = cache_k.shape[2]
    assert d == D_MODEL and seq % 1024 == 0 and (dec_batch * dec_seq) % 1024 == 0
    assert dec_seq >= HALO and seq >= HALO and dec_seq <= LANES

    nb = batch + dec_batch
    nb_pad = -(-nb // 8) * 8
    c_all = jnp.concatenate([c_prompt, c_sample, jnp.zeros((nb_pad - nb, d), F32)], axis=0)
    mod_all = _ada_call(c_all, w_ada, b_ada)

    xp = x_prompt.reshape(batch * seq, d)
    xs = x_sample.reshape(dec_batch * dec_seq, d)
    cache_k2 = cache_k.reshape(depth, dec_batch, past * N_HEADS, HEAD_DIM)
    cache_v2 = cache_v.reshape(depth, dec_batch, past * N_HEADS, HEAD_DIM)
    cache_lf_t = jnp.swapaxes(cache_logf, 2, 3).reshape(depth, dec_batch * N_HEADS, past)

    w_t = jnp.swapaxes(w_in, 1, 2)
    w_all = _prep_in_call(w_t)
    w_f = jnp.pad(w_t[:, FORGET_COL:FORGET_COL + N_HEADS, :], ((0, 0), (0, LANES - N_HEADS), (0, 0))).astype(BF16)
    woa, wob, wout = _cast_call(w_oa), _cast_call(w_ob), _cast_call(w_out)
    wl = dict(
        norm_g=norm_g.reshape(depth, 1, d), w_all=w_all, w_f=w_f,
        b_f=jnp.pad(b_f, ((0, 0), (0, LANES - N_HEADS))).reshape(depth, 1, LANES),
        q_g=q_norm_g.reshape(depth, 1, HEAD_DIM), k_g=k_norm_g.reshape(depth, 1, HEAD_DIM),
    )
    cb = conv_b.reshape(depth, 1, D_CONV)
    lg, lb = conv_ln_g.reshape(depth, 1, D_CONV), conv_ln_b.reshape(depth, 1, D_CONV)

    conv_p, conv_s = [], []
    st_p = st_s = hp = hs = None
    for l in range(depth):
        more = l + 1 < depth
        lf, q, st_p, kb, vb, ga, u, gb, mab = _layer_group(xp, hp, mod_all, seq, 0, wl, l, depth, st_p)
        f_col, f_row = _cum_p_call(lf, batch, seq)
        ya = _attn_p_call(q, kb, vb, f_col, f_row, ga, batch, seq)
        yb = _conv_call(u, u, gb, conv_w, cb, lg, lb, l, 512, seq // 512, True)
        xp, hp = _out_call(ya, yb, mab, xp, mod_all, wl["norm_g"], woa, wob, wout, l, more, seq, 0)
        conv_p.append(u.reshape(batch, seq, D_CONV)[:, seq - HALO:, :])

        lf, q, st_s, kb, vb, ga, u, gb, mab = _layer_group(xs, hs, mod_all, dec_seq, batch, wl, l, depth, st_s)
        g_row, fn_col, fn_row = _cum_s_call(cache_lf_t, lf, l, dec_batch, past, dec_seq)
        ya = _attn_s_call(q, cache_k2, cache_v2, g_row, fn_col, fn_row, kb, vb, ga,
                          l, dec_batch, past, dec_seq)
        yb = _conv_call(u, state_conv, gb, conv_w, cb, lg, lb, l, dec_seq, 1, False)
        xs, hs = _out_call(ya, yb, mab, xs, mod_all, wl["norm_g"], woa, wob, wout, l, more, dec_seq, batch)
        conv_s.append(u.reshape(dec_batch, dec_seq, D_CONV)[:, dec_seq - HALO:, :])

    cache_p = (depth, batch, seq, N_HEADS, HEAD_DIM)
    cache_s = (depth, dec_batch, dec_seq, N_HEADS, HEAD_DIM)
    return (xp.reshape(batch, seq, d), xs.reshape(dec_batch, dec_seq, d),
            st_p[1].reshape(cache_p), st_p[2].reshape(cache_p),
            st_p[0].reshape(depth, batch, seq, N_HEADS), jnp.stack(conv_p),
            st_s[1].reshape(cache_s), st_s[2].reshape(cache_s),
            st_s[0].reshape(depth, dec_batch, dec_seq, N_HEADS), jnp.stack(conv_s))
```

```python
import functools

import jax
import jax.numpy as jnp
from jax import lax
from jax.experimental import pallas as pl
from jax.experimental.pallas import tpu as pltpu

D_MODEL = 2048
N_HEADS = 8
HEAD_DIM = 128
D_ATTN = N_HEADS * HEAD_DIM
D_CONV = 1024
CONV_WIDTH = 31
HALO = CONV_WIDTH - 1
EPS = 1e-6
NEG = -1e30
LOG2E = 1.4426950408889634

LANES = 128
HALO_PAD = 32
CUM_CHUNK = 256
MIB = 1024 * 1024

BF16 = jnp.bfloat16
F32 = jnp.float32


def _params(sem, vmem_mib):
    return pltpu.CompilerParams(dimension_semantics=sem, vmem_limit_bytes=vmem_mib * MIB)


def _layer_spec(shape, layer):
    nd = len(shape) - 1
    return pl.BlockSpec((None,) + tuple(shape[1:]), lambda *_: (layer,) + (0,) * nd, pipeline_mode=pl.Buffered(1))


def _sigmoid(x):
    return 1.0 / (1.0 + jnp.exp(-x))


def _silu(x):
    return x * _sigmoid(x)


def _log_sigmoid(x):
    return jnp.minimum(x, 0.0) - jnp.log1p(jnp.exp(-jnp.abs(x)))


def _nt_dot(a, b):
    return lax.dot_general(a, b, (((1,), (1,)), ((), ())), preferred_element_type=F32)


def _dot(a, b):
    return jnp.dot(a, b, preferred_element_type=F32)


def _split3(x):
    hi = x.astype(BF16)
    r1 = x - hi.astype(F32)
    mid = r1.astype(BF16)
    lo = (r1 - mid.astype(F32)).astype(BF16)
    return hi, mid, lo


def _ada_kernel(c_ref, w_ref, b_ref, o_ref):
    s = _silu(c_ref[...]).astype(BF16)
    o_ref[...] = _dot(s, w_ref[...].astype(BF16)) + b_ref[...]


def _ada_call(c_all, w_ada, b_ada):
    depth, d, e = w_ada.shape
    nb = c_all.shape[0]
    tn = 1024
    return pl.pallas_call(
        _ada_kernel,
        grid=(depth, e // tn),
        in_specs=[
            pl.BlockSpec((nb, d), lambda l, n: (0, 0)),
            pl.BlockSpec((None, d, tn), lambda l, n: (l, 0, n)),
            pl.BlockSpec((None, 1, tn), lambda l, n: (l, 0, n)),
        ],
        out_specs=pl.BlockSpec((None, nb, tn), lambda l, n: (l, 0, n)),
        out_shape=jax.ShapeDtypeStruct((depth, nb, e), F32),
        compiler_params=_params(("parallel", "parallel"), 32),
        name="ada_mod",
    )(c_all, w_ada, b_ada.reshape(depth, 1, e))


def _batch_of(row, rows_per_batch, batch0):
    return batch0 + lax.div(row, rows_per_batch)


def _mod_norm(x, g, scale, shift):
    ms = jnp.mean(x * x, axis=-1, keepdims=True)
    y = x * lax.rsqrt(ms + EPS) * g
    return (y * (1.0 + scale) + shift).astype(BF16)


def _norm_kernel(x_ref, mod_ref, g_ref, h_ref, *, tm, rows_per_batch, batch0):
    m = pl.program_id(0)
    sub = min(tm, rows_per_batch)
    for i in range(tm // sub):
        b = _batch_of(m * tm + i * sub, rows_per_batch, batch0)
        shift = mod_ref[pl.ds(b, 1), 0:D_MODEL]
        scale = mod_ref[pl.ds(b, 1), D_MODEL:2 * D_MODEL]
        rs = slice(i * sub, (i + 1) * sub)
        h_ref[rs, :] = _mod_norm(x_ref[rs, :], g_ref[...], scale, shift)


def _norm_call(x, mod, g, layer, rows_per_batch, batch0):
    rows = x.shape[0]
    tm = 1024
    kern = functools.partial(_norm_kernel, tm=tm, rows_per_batch=rows_per_batch, batch0=batch0)
    return pl.pallas_call(
        kern,
        grid=(rows // tm,),
        in_specs=[
            pl.BlockSpec((tm, D_MODEL), lambda m: (m, 0)),
            _layer_spec(mod.shape, layer),
            _layer_spec(g.shape, layer),
        ],
        out_specs=pl.BlockSpec((tm, D_MODEL), lambda m: (m, 0)),
        out_shape=jax.ShapeDtypeStruct((rows, D_MODEL), BF16),
        compiler_params=_params(("parallel",), 32),
        name="norm_mod",
    )(x, mod, g)


def _head_rmsnorm(a, g):
    ms = jnp.mean(a * a, axis=-1, keepdims=True)
    return a * lax.rsqrt(ms + EPS) * g


def _store_heads(y, hd, tm, cache_ref, bf_ref):
    cache_ref[pl.ds(hd, tm, stride=N_HEADS), :] = y
    bf_ref[:, hd * HEAD_DIM:(hd + 1) * HEAD_DIM] = y.astype(BF16)


def _in_a_kernel(*refs, tm, n_alias):
    (h_ref, w_ref, wf_ref, bf_ref, qg_ref, kg_ref) = refs[:6]
    (lf_ref, lf8_ref, q_ref, k_ref, v_ref, kb_ref, vb_ref, ga_ref) = refs[6 + n_alias:]
    n = pl.program_id(1)

    def head(acc, hd):
        return acc[:, hd * HEAD_DIM:(hd + 1) * HEAD_DIM]

    @pl.when(n == 0)
    def _():
        f = _nt_dot(h_ref[...], wf_ref[...]) + bf_ref[...]
        lf = _log_sigmoid(f)
        lf_ref[...] = lf
        lf8_ref[...] = lf[:, 0:N_HEADS]
        acc = _nt_dot(h_ref[...], w_ref[...])
        for hd in range(N_HEADS):
            y = _head_rmsnorm(head(acc, hd), qg_ref[...]) * (HEAD_DIM ** -0.5 * LOG2E)
            q_ref[:, hd * HEAD_DIM:(hd + 1) * HEAD_DIM] = y.astype(BF16)

    @pl.when(n == 1)
    def _():
        acc = _nt_dot(h_ref[...], w_ref[...])
        for hd in range(N_HEADS):
            _store_heads(_head_rmsnorm(head(acc, hd), kg_ref[...]), hd, tm, k_ref, kb_ref)

    @pl.when(n == 2)
    def _():
        acc = _nt_dot(h_ref[...], w_ref[...])
        for hd in range(N_HEADS):
            _store_heads(head(acc, hd), hd, tm, v_ref, vb_ref)

    @pl.when(n == 3)
    def _():
        ga_ref[...] = _silu(_nt_dot(h_ref[...], w_ref[...])).astype(BF16)


def _in_a_call(h, w_all, w_f, b_f, q_g, k_g, layer, depth, stacked_prev):
    rows = h.shape[0]
    tm, tn = 1024, 1024
    row_blk = lambda width: pl.BlockSpec((tm, width), lambda m, n: (m, 0))
    cache_blk = pl.BlockSpec((None, tm * N_HEADS, HEAD_DIM), lambda m, n: (layer, m, 0))
    cache_shape = jax.ShapeDtypeStruct((depth, rows * N_HEADS, HEAD_DIM), F32)
    alias_in = [] if stacked_prev is None else list(stacked_prev)
    n_in = 6
    return pl.pallas_call(
        functools.partial(_in_a_kernel, tm=tm, n_alias=len(alias_in)),
        grid=(rows // tm, 4),
        in_specs=[
            row_blk(D_MODEL),
            pl.BlockSpec((None, tn, D_MODEL), lambda m, n: (layer, n, 0)),
            _layer_spec(w_f.shape, layer),
            _layer_spec(b_f.shape, layer),
            _layer_spec(q_g.shape, layer),
            _layer_spec(k_g.shape, layer),
        ] + [pl.BlockSpec(memory_space=pl.ANY)] * len(alias_in),
        out_specs=[row_blk(LANES),
                   pl.BlockSpec((None, tm, N_HEADS), lambda m, n: (layer, m, 0)),
                   row_blk(D_ATTN), cache_blk, cache_blk,
                   row_blk(D_ATTN), row_blk(D_ATTN), row_blk(D_ATTN)],
        out_shape=[
            jax.ShapeDtypeStruct((rows, LANES), F32),
            jax.ShapeDtypeStruct((depth, rows, N_HEADS), F32),
            jax.ShapeDtypeStruct((rows, D_ATTN), BF16),
            cache_shape,
            cache_shape,
            jax.ShapeDtypeStruct((rows, D_ATTN), BF16),
            jax.ShapeDtypeStruct((rows, D_ATTN), BF16),
            jax.ShapeDtypeStruct((rows, D_ATTN), BF16),
        ],
        input_output_aliases={n_in + i: out for i, out in zip(range(len(alias_in)), (1, 3, 4))},
        compiler_params=_params(("parallel", "arbitrary"), 58),
        name="in_proj_a",
    )(h, w_all, w_f, b_f, q_g, k_g, *alias_in)


IN_A_TILES = 4


def _in_b_kernel(h_ref, w_ref, u_ref, gb_ref, mab_ref, lin_sc):
    n = pl.program_id(1)

    @pl.when(n == 0)
    def _():
        lin_sc[...] = _nt_dot(h_ref[...], w_ref[...])

    @pl.when(n == 1)
    def _():
        u_ref[...] = lin_sc[...] * _sigmoid(_nt_dot(h_ref[...], w_ref[...]))

    @pl.when(n == 2)
    def _():
        gb_ref[...] = _silu(_nt_dot(h_ref[...], w_ref[...])).astype(BF16)

    @pl.when(n > 2)
    def _():
        mab_ref[...] = _sigmoid(_nt_dot(h_ref[...], w_ref[...])).astype(BF16)


def _in_b_call(h, w_all, layer):
    rows = h.shape[0]
    tm, tn = 1024, 1024
    return pl.pallas_call(
        _in_b_kernel,
        grid=(rows // tm, 7),
        in_specs=[
            pl.BlockSpec((tm, D_MODEL), lambda m, n: (m, 0)),
            pl.BlockSpec((None, tn, D_MODEL), lambda m, n: (layer, n + IN_A_TILES, 0)),
        ],
        out_specs=[
            pl.BlockSpec((tm, D_CONV), lambda m, n: (m, 0)),
            pl.BlockSpec((tm, D_CONV), lambda m, n: (m, 0)),
            pl.BlockSpec((tm, tn), lambda m, n: (m, jnp.clip(n - 3, 0, 3))),
        ],
        out_shape=[
            jax.ShapeDtypeStruct((rows, D_CONV), F32),
            jax.ShapeDtypeStruct((rows, D_CONV), BF16),
            jax.ShapeDtypeStruct((rows, 2 * D_MODEL), BF16),
        ],
        scratch_shapes=[pltpu.VMEM((tm, D_CONV), F32)],
        compiler_params=_params(("parallel", "arbitrary"), 48),
        name="in_proj_b",
    )(h, w_all)


def _tri(n, upper):
    r = lax.broadcasted_iota(jnp.int32, (n, n), 0)
    c = lax.broadcasted_iota(jnp.int32, (n, n), 1)
    keep = (r <= c) if upper else (c <= r)
    return jnp.where(keep, 1.0, 0.0).astype(BF16)


def _cum_p_kernel(lf_ref, fc_ref, fr_ref, *, seq):
    tril = _tri(CUM_CHUNK, upper=False)
    carry = jnp.zeros((1, LANES), F32)
    for i in range(seq // CUM_CHUNK):
        sl = slice(i * CUM_CHUNK, (i + 1) * CUM_CHUNK)
        hi, mid, lo = _split3(lf_ref[sl, :])
        y = _dot(tril, hi) + _dot(tril, mid) + _dot(tril, lo) + carry
        carry = y[CUM_CHUNK - 1:CUM_CHUNK, :]
        y2 = y * LOG2E
        fc_ref[sl, :] = y2
        fr_ref[:, sl] = y2.T[0:N_HEADS, :]


def _cum_p_call(lf, batch, seq):
    kern = functools.partial(_cum_p_kernel, seq=seq)
    return pl.pallas_call(
        kern,
        grid=(batch,),
        in_specs=[pl.BlockSpec((seq, LANES), lambda b: (b, 0))],
        out_specs=[
            pl.BlockSpec((seq, LANES), lambda b: (b, 0)),
            pl.BlockSpec((None, N_HEADS, seq), lambda b: (b, 0, 0)),
        ],
        out_shape=[
            jax.ShapeDtypeStruct((batch * seq, LANES), F32),
            jax.ShapeDtypeStruct((batch, N_HEADS, seq), F32),
        ],
        compiler_params=_params(("parallel",), 32),
        name="cum_prompt",
    )(lf)


def _cum_s_kernel(clf_ref, lfn_ref, g_ref, fnc_ref, fnr_ref, f_sc, *, batch, past, new):
    triu = _tri(CUM_CHUNK, upper=True)
    carry = jnp.zeros((batch * N_HEADS, 1), F32)
    for i in range(past // CUM_CHUNK):
        sl = slice(i * CUM_CHUNK, (i + 1) * CUM_CHUNK)
        hi, mid, lo = _split3(clf_ref[:, sl])
        y = _dot(hi, triu) + _dot(mid, triu) + _dot(lo, triu) + carry
        carry = y[:, CUM_CHUNK - 1:CUM_CHUNK]
        f_sc[:, sl] = y
    g_ref[...] = (carry - f_sc[...]) * LOG2E
    tril = _tri(LANES, upper=False)
    pad = jnp.zeros((LANES - new, LANES), F32)
    for b in range(batch):
        rows = slice(b * new, (b + 1) * new)
        hi, mid, lo = _split3(jnp.concatenate([lfn_ref[rows, :], pad], axis=0))
        yn = (_dot(tril, hi) + _dot(tril, mid) + _dot(tril, lo)) * LOG2E
        fnc_ref[rows, :] = yn[0:new, :]
        fnr_ref[b] = yn.T[0:N_HEADS, :]


def _cum_s_call(clf_t, lf_new, layer, batch, past, new):
    kern = functools.partial(_cum_s_kernel, batch=batch, past=past, new=new)
    bh = batch * N_HEADS
    g_rows, fn_col, fn_row = pl.pallas_call(
        kern,
        grid=(1,),
        in_specs=[
            pl.BlockSpec((None, bh, past), lambda i: (layer, 0, 0)),
            pl.BlockSpec((batch * new, LANES), lambda i: (0, 0)),
        ],
        out_specs=[
            pl.BlockSpec((bh, past), lambda i: (0, 0)),
            pl.BlockSpec((batch * new, LANES), lambda i: (0, 0)),
            pl.BlockSpec((batch, N_HEADS, LANES), lambda i: (0, 0, 0)),
        ],
        out_shape=[
            jax.ShapeDtypeStruct((bh, past), F32),
            jax.ShapeDtypeStruct((batch * new, LANES), F32),
            jax.ShapeDtypeStruct((batch, N_HEADS, LANES), F32),
        ],
        scratch_shapes=[pltpu.VMEM((bh, past), F32)],
        compiler_params=_params(("arbitrary",), 32),
        name="cum_sample",
    )(clf_t, lf_new)
    return g_rows.reshape(batch, N_HEADS, past), fn_col, fn_row


def _softmax_parts(carry, t, fq):
    m, l, acc = carry
    m_new = jnp.maximum(m, jnp.max(t, axis=-1, keepdims=True) + fq)
    alpha = jnp.exp2(m - m_new)
    p = jnp.exp2(t + (fq - m_new))
    return m_new, alpha * l + jnp.sum(p, axis=-1, keepdims=True), alpha * acc, p.astype(BF16)


def _softmax_step(carry, t, fq, v):
    m_new, l, acc, p = _softmax_parts(carry, t, fq)
    return m_new, l, acc + _dot(p, v)


def _causal(s):
    row = lax.broadcasted_iota(jnp.int32, s.shape, 0)
    col = lax.broadcasted_iota(jnp.int32, s.shape, 1)
    return jnp.where(col <= row, s, NEG)


def _attn_p_kernel(q_ref, k_ref, v_ref, fc_ref, fr_ref, ga_ref, o_ref, *, seq, tq):
    hd = pl.program_id(1)
    nq = seq // tq
    lane = lax.broadcasted_iota(jnp.int32, (seq, LANES), 1)
    fq_all = jnp.sum(jnp.where(lane == hd, fc_ref[...], 0.0), axis=-1, keepdims=True)
    fr = fr_ref[pl.ds(hd, 1), :]
    blk = lambda i: slice(i * tq, (i + 1) * tq)

    def scores(i, j):
        t = _nt_dot(q_ref[blk(i), :], k_ref[blk(j), :]) - fr[:, blk(j)]
        return _causal(t) if i == j else t

    pairs = [(i, j) for i in range(nq) for j in range(i + 1)]
    t_next = scores(*pairs[0])
    carry = None
    for n, (i, j) in enumerate(pairs):
        t = t_next
        if n + 1 < len(pairs):
            t_next = scores(*pairs[n + 1])
        if j == 0:
            carry = (jnp.full((tq, 1), NEG, F32), jnp.zeros((tq, 1), F32), jnp.zeros((tq, HEAD_DIM), F32))
        carry = _softmax_step(carry, t, fq_all[blk(i), :], v_ref[blk(j), :])
        if j == i:
            _, l, acc = carry
            o_ref[blk(i), :] = (acc / l * ga_ref[blk(i), :].astype(F32)).astype(BF16)


def _attn_p_call(q, k, v, f_col, f_row, ga, batch, seq):
    tq = 512
    head = pl.BlockSpec((seq, HEAD_DIM), lambda b, h: (b, h))
    return pl.pallas_call(
        functools.partial(_attn_p_kernel, seq=seq, tq=tq),
        grid=(batch, N_HEADS),
        in_specs=[
            head, head, head,
            pl.BlockSpec((seq, LANES), lambda b, h: (b, 0)),
            pl.BlockSpec((None, N_HEADS, seq), lambda b, h: (b, 0, 0)),
            head,
        ],
        out_specs=head,
        out_shape=jax.ShapeDtypeStruct((batch * seq, D_ATTN), BF16),
        compiler_params=_params(("parallel", "parallel"), 32),
        name="attn_prompt",
    )(q, k, v, f_col, f_row, ga)


def _attn_s_kernel(q_ref, ck_ref, cv_ref, g_ref, fnc_ref, fnr_ref, kn_ref, vn_ref, ga_ref,
                   o_ref, m_sc, l_sc, acc_sc, *, new, tk):
    j = pl.program_id(1)
    last = pl.num_programs(1) - 1

    @pl.when(j == 0)
    def _():
        m_sc[...] = jnp.full(m_sc.shape, NEG, F32)
        l_sc[...] = jnp.zeros(l_sc.shape, F32)
        acc_sc[...] = jnp.zeros(acc_sc.shape, F32)

    heads = range(N_HEADS)
    sls = [slice(hd * HEAD_DIM, (hd + 1) * HEAD_DIM) for hd in heads]
    state = [(m_sc[hd][:, 0:1], l_sc[hd][:, 0:1], acc_sc[:, sls[hd]]) for hd in heads]
    fq = [fnc_ref[:, hd:hd + 1] for hd in heads]

    ts = [_nt_dot(q_ref[:, sls[hd]], ck_ref[pl.ds(hd, tk, stride=N_HEADS), :].astype(BF16)) + g_ref[hd:hd + 1, :]
          for hd in heads]
    parts = [_softmax_parts(state[hd], ts[hd], fq[hd]) for hd in heads]
    out = []
    for hd in heads:
        m, l, acc, p = parts[hd]
        out.append((m, l, acc + _dot(p, cv_ref[pl.ds(hd, tk, stride=N_HEADS), :].astype(BF16))))
    for hd in heads:
        m, l, acc = out[hd]
        m_sc[hd] = jnp.broadcast_to(m, (new, LANES))
        l_sc[hd] = jnp.broadcast_to(l, (new, LANES))
        acc_sc[:, sls[hd]] = acc

    @pl.when(j == last)
    def _():
        pad = jnp.zeros((LANES - new, HEAD_DIM), BF16)
        ts = [_causal(_nt_dot(q_ref[:, sls[hd]], jnp.concatenate([kn_ref[:, sls[hd]], pad], axis=0))
                      - fnr_ref[hd:hd + 1, :]) for hd in heads]
        parts = [_softmax_parts(out[hd], ts[hd], fq[hd]) for hd in heads]
        for hd in heads:
            _, l, acc, p = parts[hd]
            acc = acc + _dot(p, jnp.concatenate([vn_ref[:, sls[hd]], pad], axis=0))
            o_ref[:, sls[hd]] = (acc / l * ga_ref[:, sls[hd]].astype(F32)).astype(BF16)


def _attn_s_call(q, cache_k, cache_v, g_row, fn_col, fn_row, k_new, v_new, ga, layer, batch, past, new):
    tk = 2048
    row = lambda width: pl.BlockSpec((new, width), lambda b, j: (b, 0))
    cache = pl.BlockSpec((None, None, tk * N_HEADS, HEAD_DIM), lambda b, j: (layer, b, j, 0))
    return pl.pallas_call(
        functools.partial(_attn_s_kernel, new=new, tk=tk),
        grid=(batch, past // tk),
        in_specs=[
            row(D_ATTN), cache, cache,
            pl.BlockSpec((None, N_HEADS, tk), lambda b, j: (b, 0, j)),
            row(LANES),
            pl.BlockSpec((None, N_HEADS, LANES), lambda b, j: (b, 0, 0)),
            row(D_ATTN), row(D_ATTN), row(D_ATTN),
        ],
        out_specs=row(D_ATTN),
        out_shape=jax.ShapeDtypeStruct((batch * new, D_ATTN), BF16),
        scratch_shapes=[
            pltpu.VMEM((N_HEADS, new, LANES), F32),
            pltpu.VMEM((N_HEADS, new, LANES), F32),
            pltpu.VMEM((new, D_ATTN), F32),
        ],
        compiler_params=_params(("parallel", "arbitrary"), 52),
        name="attn_sample",
    )(q, cache_k, cache_v, g_row, fn_col, fn_row, k_new, v_new, ga)


CONV_ROWS = 128


def _conv_kernel(u_ref, halo_ref, gb_ref, w_ref, cb_ref, lg_ref, lb_ref, o_ref, ext_sc, cv_sc,
                 *, tt, tiles_per_batch, halo_from_u):
    nc = D_CONV // LANES
    chunk = lambda c: slice(c * LANES, (c + 1) * LANES)
    for c in range(nc):
        ext_sc[c, HALO_PAD:HALO_PAD + tt, :] = u_ref[:, chunk(c)]
    if halo_from_u:
        first = lax.rem(pl.program_id(0), tiles_per_batch) == 0

        @pl.when(first)
        def _():
            ext_sc[:, 0:HALO_PAD, :] = jnp.zeros((nc, HALO_PAD, LANES), F32)

        @pl.when(jnp.logical_not(first))
        def _():
            for c in range(nc):
                ext_sc[c, 0:HALO_PAD, :] = halo_ref[:, chunk(c)]
    else:
        for c in range(nc):
            ext_sc[c, HALO_PAD - HALO:HALO_PAD, :] = halo_ref[:, chunk(c)]

    base = HALO_PAD - HALO
    grp = min(CONV_ROWS, tt)
    for c in range(nc):
        w_taps = w_ref[:, chunk(c)]
        bias = cb_ref[:, chunk(c)]

        def rows_body(r, carry, c=c, w_taps=w_taps, bias=bias):
            r0 = pl.multiple_of(r * grp, grp)
            acc = jnp.zeros((grp, LANES), F32)
            for j in range(CONV_WIDTH):
                acc = acc + w_taps[j:j + 1, :] * ext_sc[c, pl.ds(r0 + (base + j), grp, stride=1), :]
            cv_sc[pl.ds(r0, grp), chunk(c)] = acc + bias
            return carry

        lax.fori_loop(0, tt // grp, rows_body, 0)

    cv = cv_sc[...]
    mu = jnp.mean(cv, axis=-1, keepdims=True)
    d = cv - mu
    var = jnp.mean(d * d, axis=-1, keepdims=True)
    y = d * lax.rsqrt(var + EPS) * lg_ref[...] + lb_ref[...]
    o_ref[...] = (_silu(y) * gb_ref[...].astype(F32)).astype(BF16)


def _conv_call(u, halo, gb, conv_w, conv_b, ln_g, ln_b, layer, tt, tiles_per_batch, halo_from_u):
    rows = u.shape[0]
    tile = pl.BlockSpec((tt, D_CONV), lambda t: (t, 0))
    if halo_from_u:
        per = tt // HALO_PAD
        halo_spec = pl.BlockSpec((HALO_PAD, D_CONV), lambda t: (jnp.maximum(t * per - 1, 0), 0))
    else:
        halo_spec = pl.BlockSpec((None, None, HALO, D_CONV), lambda t: (layer, t, 0, 0))
    kern = functools.partial(_conv_kernel, tt=tt, tiles_per_batch=tiles_per_batch, halo_from_u=halo_from_u)
    return pl.pallas_call(
        kern,
        grid=(rows // tt,),
        in_specs=[
            tile, halo_spec, tile,
            _layer_spec(conv_w.shape, layer), _layer_spec(conv_b.shape, layer),
            _layer_spec(ln_g.shape, layer), _layer_spec(ln_b.shape, layer),
        ],
        out_specs=tile,
        out_shape=jax.ShapeDtypeStruct((rows, D_CONV), BF16),
        scratch_shapes=[
            pltpu.VMEM((D_CONV // LANES, HALO_PAD + tt, LANES), F32),
            pltpu.VMEM((tt, D_CONV), F32),
        ],
        compiler_params=_params(("parallel",), 32),
        name="conv_module",
    )(u, halo, gb, conv_w, conv_b, ln_g, ln_b)


OUT_COLS = 512


def _out_kernel(*refs, tm, rows_per_batch, batch0, next_norm):
    ya_ref, yb_ref, mab_ref, x_ref, mod_ref, woa_ref, wob_ref, wout_ref = refs[:8]
    if next_norm:
        modn_ref, gn_ref, o_ref, hn_ref, mg_sc = refs[8:]
    else:
        o_ref, mg_sc = refs[8:]
    m = pl.program_id(0)
    ya = ya_ref[...]
    yb = yb_ref[...]
    for c in range(D_MODEL // OUT_COLS):
        cs = slice(c * OUT_COLS, (c + 1) * OUT_COLS)
        cs_b = slice(D_MODEL + c * OUT_COLS, D_MODEL + (c + 1) * OUT_COLS)
        pa = _dot(ya, woa_ref[:, cs])
        pb = _dot(yb, wob_ref[:, cs])
        merged = mab_ref[:, cs].astype(F32) * pa + mab_ref[:, cs_b].astype(F32) * pb
        mg_sc[:, cs] = merged.astype(BF16)
    out = _dot(mg_sc[...], wout_ref[...])
    sub = min(tm, rows_per_batch)
    for i in range(tm // sub):
        b = _batch_of(m * tm + i * sub, rows_per_batch, batch0)
        gate = mod_ref[pl.ds(b, 1), 2 * D_MODEL:3 * D_MODEL]
        rs = slice(i * sub, (i + 1) * sub)
        x_new = x_ref[rs, :] + gate * out[rs, :]
        o_ref[rs, :] = x_new
        if next_norm:
            shift = modn_ref[pl.ds(b, 1), 0:D_MODEL]
            scale = modn_ref[pl.ds(b, 1), D_MODEL:2 * D_MODEL]
            hn_ref[rs, :] = _mod_norm(x_new, gn_ref[...], scale, shift)


def _out_call(ya, yb, mab, x, mod, norm_g, w_oa, w_ob, w_out, layer, next_norm, rows_per_batch, batch0):
    rows = x.shape[0]
    tm = 512
    row_blk = lambda width: pl.BlockSpec((tm, width), lambda m: (m, 0))
    kern = functools.partial(_out_kernel, tm=tm, rows_per_batch=rows_per_batch, batch0=batch0, next_norm=next_norm)
    in_specs = [
        row_blk(D_ATTN), row_blk(D_CONV), row_blk(2 * D_MODEL), row_blk(D_MODEL),
        _layer_spec(mod.shape, layer), _layer_spec(w_oa.shape, layer),
        _layer_spec(w_ob.shape, layer), _layer_spec(w_out.shape, layer),
    ]
    args = [ya, yb, mab, x, mod, w_oa, w_ob, w_out]
    out_specs = [row_blk(D_MODEL)]
    out_shape = [jax.ShapeDtypeStruct((rows, D_MODEL), F32)]
    if next_norm:
        in_specs += [_layer_spec(mod.shape, layer + 1), _layer_spec(norm_g.shape, layer + 1)]
        args += [mod, norm_g]
        out_specs.append(row_blk(D_MODEL))
        out_shape.append(jax.ShapeDtypeStruct((rows, D_MODEL), BF16))
    res = pl.pallas_call(
        kern,
        grid=(rows // tm,),
        in_specs=in_specs,
        out_specs=out_specs,
        out_shape=out_shape,
        scratch_shapes=[pltpu.VMEM((tm, D_MODEL), BF16)],
        compiler_params=_params(("parallel",), 58),
        name="out_proj",
    )(*args)
    return (res[0], res[1]) if next_norm else (res[0], None)


PREP_TILE = 1024
FORGET_COL = 3 * D_ATTN


def _prep_in_kernel(a_ref, b_ref, o_ref):
    i = pl.program_id(1)

    @pl.when(i < FORGET_COL // PREP_TILE)
    def _():
        o_ref[...] = a_ref[...].astype(BF16)

    @pl.when(i >= FORGET_COL // PREP_TILE)
    def _():
        o_ref[0:PREP_TILE - N_HEADS, :] = a_ref[N_HEADS:PREP_TILE, :].astype(BF16)
        o_ref[PREP_TILE - N_HEADS:PREP_TILE, :] = b_ref[...].astype(BF16)


def _prep_in_call(w_t):
    depth, d_in, d = w_t.shape
    n_tiles = (d_in - N_HEADS) // PREP_TILE
    per = PREP_TILE // N_HEADS
    return pl.pallas_call(
        _prep_in_kernel,
        grid=(depth, n_tiles),
        in_specs=[
            pl.BlockSpec((None, PREP_TILE, d), lambda l, i: (l, i, 0)),
            pl.BlockSpec((None, N_HEADS, d), lambda l, i: (l, (i + 1) * per, 0)),
        ],
        out_specs=pl.BlockSpec((None, PREP_TILE, d), lambda l, i: (l, i, 0)),
        out_shape=jax.ShapeDtypeStruct((depth, n_tiles * PREP_TILE, d), BF16),
        compiler_params=_params(("parallel", "parallel"), 40),
        name="prep_w_in",
    )(w_t, w_t)


def _cast_kernel(x_ref, o_ref):
    o_ref[...] = x_ref[...].astype(BF16)


def _cast_call(w):
    depth, k, n = w.shape
    rows = 512
    blk = pl.BlockSpec((None, rows, n), lambda l, r: (l, r, 0))
    return pl.pallas_call(
        _cast_kernel,
        grid=(depth, k // rows),
        in_specs=[blk],
        out_specs=blk,
        out_shape=jax.ShapeDtypeStruct(w.shape, BF16),
        compiler_params=_params(("parallel", "parallel"), 32),
        name="cast_bf16",
    )(w)


def _layer_group(x, h, mod, rows_per_batch, batch0, wl, layer, depth, stacked_prev):
    if h is None:
        h = _norm_call(x, mod, wl["norm_g"], layer, rows_per_batch, batch0)
    lf, lf_all, q, k_all, v_all, kb, vb, ga = _in_a_call(
        h, wl["w_all"], wl["w_f"], wl["b_f"], wl["q_g"], wl["k_g"], layer, depth, stacked_prev)
    u, gb, mab = _in_b_call(h, wl["w_all"], layer)
    return lf, q, (lf_all, k_all, v_all), kb, vb, ga, u, gb, mab


def kernel(x_prompt, x_sample, c_prompt, c_sample, cache_k, cache_v, cache_logf, state_conv, w_ada, b_ada, norm_g, w_in, b_f, q_norm_g, k_norm_g, conv_w, conv_b, conv_ln_g, conv_ln_b, w_oa, w_ob, w_out):
    batch, seq, d = x_prompt.shape
    dec_batch, dec_seq, _ = x_sample.shape
    depth = w_ada.shape[0]
    past = cache_k.shape[2]
    assert d == D_MODEL and seq % 1024 == 0 and (dec_batch * dec_seq) % 1024 == 0
    assert dec_seq >= HALO and seq >= HALO and dec_seq <= LANES

    nb = batch + dec_batch
    nb_pad = -(-nb // 8) * 8
    c_all = jnp.concatenate([c_prompt, c_sample, jnp.zeros((nb_pad - nb, d), F32)], axis=0)
    mod_all = _ada_call(c_all, w_ada, b_ada)

    xp = x_prompt.reshape(batch * seq, d)
    xs = x_sample.reshape(dec_batch * dec_seq, d)
    cache_k2 = cache_k.reshape(depth, dec_batch, past * N_HEADS, HEAD_DIM)
    cache_v2 = cache_v.reshape(depth, dec_batch, past * N_HEADS, HEAD_DIM)
    cache_lf_t = jnp.swapaxes(cache_logf, 2, 3).reshape(depth, dec_batch * N_HEADS, past)

    w_t = jnp.swapaxes(w_in, 1, 2)
    w_all = _prep_in_call(w_t)
    w_f = jnp.pad(w_t[:, FORGET_COL:FORGET_COL + N_HEADS, :], ((0, 0), (0, LANES - N_HEADS), (0, 0))).astype(BF16)
    woa, wob, wout = _cast_call(w_oa), _cast_call(w_ob), _cast_call(w_out)
    wl = dict(
        norm_g=norm_g.reshape(depth, 1, d), w_all=w_all, w_f=w_f,
        b_f=jnp.pad(b_f, ((0, 0), (0, LANES - N_HEADS))).reshape(depth, 1, LANES),
        q_g=q_norm_g.reshape(depth, 1, HEAD_DIM), k_g=k_norm_g.reshape(depth, 1, HEAD_DIM),
    )
    cb = conv_b.reshape(depth, 1, D_CONV)
    lg, lb = conv_ln_g.reshape(depth, 1, D_CONV), conv_ln_b.reshape(depth, 1, D_CONV)

    conv_p, conv_s = [], []
    st_p = st_s = hp = hs = None
    for l in range(depth):
        more = l + 1 < depth
        lf, q, st_p, kb, vb, ga, u, gb, mab = _layer_group(xp, hp, mod_all, seq, 0, wl, l, depth, st_p)
        f_col, f_row = _cum_p_call(lf, batch, seq)
        ya = _attn_p_call(q, kb, vb, f_col, f_row, ga, batch, seq)
        yb = _conv_call(u, u, gb, conv_w, cb, lg, lb, l, 512, seq // 512, True)
        xp, hp = _out_call(ya, yb, mab, xp, mod_all, wl["norm_g"], woa, wob, wout, l, more, seq, 0)
        conv_p.append(u.reshape(batch, seq, D_CONV)[:, seq - HALO:, :])

        lf, q, st_s, kb, vb, ga, u, gb, mab = _layer_group(xs, hs, mod_all, dec_seq, batch, wl, l, depth, st_s)
        g_row, fn_col, fn_row = _cum_s_call(cache_lf_t, lf, l, dec_batch, past, dec_seq)
        ya = _attn_s_call(q, cache_k2, cache_v2, g_row, fn_col, fn_row, kb, vb, ga,
                          l, dec_batch, past, dec_seq)
        yb = _conv_call(u, state_conv, gb, conv_w, cb, lg, lb, l, dec_seq, 1, False)
        xs, hs = _out_call(ya, yb, mab, xs, mod_all, wl["norm_g"], woa, wob, wout, l, more, dec_seq, batch)
        conv_s.append(u.reshape(dec_batch, dec_seq, D_CONV)[:, dec_seq - HALO:, :])

    cache_p = (depth, batch, seq, N_HEADS, HEAD_DIM)
    cache_s = (depth, dec_batch, dec_seq, N_HEADS, HEAD_DIM)
    return (xp.reshape(batch, seq, d), xs.reshape(dec_batch, dec_seq, d),
            st_p[1].reshape(cache_p), st_p[2].reshape(cache_p),
            st_p[0].reshape(depth, batch, seq, N_HEADS), jnp.stack(conv_p),
            st_s[1].reshape(cache_s), st_s[2].reshape(cache_s),
            st_s[0].reshape(depth, dec_batch, dec_seq, N_HEADS), jnp.stack(conv_s))
```
